```python
import math, functools
import jax, jax.numpy as jnp
from jax import lax
import numpy as np

D_MODEL = 1024
BATCH = 8
SEQ = 4096
DEPTH = 4

GRID_W = 64
CTX_LEN = 256
N_MIXERS = 3
HEAD_DIM = 64
N_HEADS = D_MODEL // HEAD_DIM
GQA_KV_HEADS = N_HEADS // 4
GQA_REP = N_HEADS // GQA_KV_HEADS
DIFF_HEADS = D_MODEL // (2 * HEAD_DIM)
NA_WIN_H = 8
NA_WIN_W = 16
D_FF = 128 * ((8 * D_MODEL // 3 + 127) // 128)
ROPE_THETA = 10000.0
Q_BLOCK = 128
N_MOD = 9
EPS = 1e-6
N_NA = (DEPTH + 2) // 3
N_GQA = (DEPTH + 1) // 3
N_DIFF = DEPTH // 3

kernel_name = "hybrid_natten_gqa_diffattn_macaron_dit"


def rms_norm(x, g):
    xf = x.astype(jnp.float32)
    y = xf * lax.rsqrt(jnp.mean(xf * xf, axis=-1, keepdims=True) + EPS)
    return (y * g.astype(jnp.float32)).astype(x.dtype)


def pre(h, g, shift, scale):
    return rms_norm(h, g) * (1 + scale) + shift


def post_add(h, y, g, gate, weight):
    return h + weight * gate * rms_norm(y, g)


def swiglu(x, w_in, w_out):
    a, b = jnp.split(x @ w_in, 2, axis=-1)
    return (jax.nn.silu(a) * b) @ w_out


def rope_tables(n, dtype):
    t = jnp.arange(n, dtype=jnp.int32)
    row = (t // GRID_W).astype(jnp.float32)
    col = (t % GRID_W).astype(jnp.float32)
    nf = HEAD_DIM // 4
    inv = 1.0 / (ROPE_THETA ** (jnp.arange(nf, dtype=jnp.float32) / nf))
    ar = row[:, None] * inv
    ac = col[:, None] * inv
    cos = jnp.concatenate([jnp.cos(ar), jnp.cos(ar), jnp.cos(ac), jnp.cos(ac)], axis=-1)
    sin = jnp.concatenate([jnp.sin(ar), jnp.sin(ar), jnp.sin(ac), jnp.sin(ac)], axis=-1)
    return cos.astype(dtype), sin.astype(dtype)


def apply_rope(x, cos, sin):
    a, b, c2, d = jnp.split(x, 4, axis=-1)
    rot = jnp.concatenate([-b, a, -d, c2], axis=-1)
    shape = (1, cos.shape[0]) + (1,) * (x.ndim - 3) + (HEAD_DIM,)
    return x * cos.reshape(shape) + rot * sin.reshape(shape)


def sweep_query_blocks(fn, q):
    B, S = q.shape[:2]
    nb = S // Q_BLOCK
    qb = jnp.moveaxis(q.reshape((B, nb, Q_BLOCK) + q.shape[2:]), 1, 0)
    out = lax.map(fn, qb)
    return jnp.moveaxis(out, 0, 1).reshape((B, S) + out.shape[3:])


def gqa_attend(q, k, v):
    s = jnp.einsum('bqgrd,bkgd->bgrqk', q, k).astype(jnp.float32) * (HEAD_DIM ** -0.5)
    p = jax.nn.softmax(s, axis=-1).astype(v.dtype)
    return jnp.einsum('bgrqk,bkgd->bqgrd', p, v)


def natten_mixer(xl, xc, w_qkv, rpb, w_o, need_ctx):
    B, S, D = xl.shape
    L = xc.shape[1]
    rows = S // GRID_W
    kh = min(NA_WIN_H, rows)
    kw = NA_WIN_W
    n_nb = kh * GRID_W
    scale = HEAD_DIM ** -0.5
    qkv = (xl @ w_qkv).reshape(B, S, 3, N_HEADS, HEAD_DIM)
    q, k, v = qkv[:, :, 0], qkv[:, :, 1], qkv[:, :, 2]
    kvc = (xc @ w_qkv[:, D:]).reshape(B, L, 2, N_HEADS, HEAD_DIM)
    kc, vc = kvc[:, :, 0], kvc[:, :, 1]
    kg = k.reshape(B, rows, GRID_W, N_HEADS, HEAD_DIM)
    vg = v.reshape(B, rows, GRID_W, N_HEADS, HEAD_DIM)
    q_rows = jnp.moveaxis(q.reshape(B, rows, GRID_W, N_HEADS, HEAD_DIM), 1, 0)
    cpos = jnp.arange(GRID_W, dtype=jnp.int32)
    cstart = jnp.clip(cpos - kw // 2, 0, GRID_W - kw)
    col_ok = (cpos[None, :] >= cstart[:, None]) & (cpos[None, :] < cstart[:, None] + kw)
    mask = jnp.tile(col_ok, (1, kh))
    cidx = jnp.clip(cpos[None, :] - cpos[:, None] + NA_WIN_W - 1, 0, 2 * NA_WIN_W - 2)

    def row_block(args):
        qr, r = args
        rs = jnp.clip(r - kh // 2, 0, rows - kh)
        kb = lax.dynamic_slice_in_dim(kg, rs, kh, axis=1).reshape(B, n_nb, N_HEADS, HEAD_DIM)
        vb = lax.dynamic_slice_in_dim(vg, rs, kh, axis=1).reshape(B, n_nb, N_HEADS, HEAD_DIM)
        ridx = rs + jnp.arange(kh, dtype=jnp.int32) - r + NA_WIN_H - 1
        bias = rpb[:, ridx][:, :, cidx]
        bias = jnp.transpose(bias, (0, 2, 1, 3)).reshape(N_HEADS, GRID_W, n_nb)
        s_nb = jnp.einsum('bqhd,bkhd->bhqk', qr, kb).astype(jnp.float32) * scale + bias.astype(jnp.float32)
        s_nb = jnp.where(mask, s_nb, -jnp.inf)
        s_cx = jnp.einsum('bqhd,bkhd->bhqk', qr, kc).astype(jnp.float32) * scale
        p = jax.nn.softmax(jnp.concatenate([s_nb, s_cx], axis=-1), axis=-1).astype(vb.dtype)
        return (jnp.einsum('bhqk,bkhd->bqhd', p[..., :n_nb], vb)
                + jnp.einsum('bhqk,bkhd->bqhd', p[..., n_nb:], vc))

    o = lax.map(row_block, (q_rows, jnp.arange(rows, dtype=jnp.int32)))
    yl = jnp.moveaxis(o, 0, 1).reshape(B, S, D) @ w_o
    yc = None
    if need_ctx:
        qc = (xc @ w_qkv[:, :D]).reshape(B, L, N_HEADS, 1, HEAD_DIM)
        yc = gqa_attend(qc, kc, vc).reshape(B, L, D) @ w_o
    return yl, yc


def gqa_mixer(xl, xc, w_qkv, q_gain, k_gain, w_o, cos, sin, need_ctx):
    B, S, D = xl.shape
    L = xc.shape[1]
    dq = N_HEADS * HEAD_DIM
    dkv = GQA_KV_HEADS * HEAD_DIM
    qkv = xl @ w_qkv
    q = rms_norm(qkv[..., :dq].reshape(B, S, GQA_KV_HEADS, GQA_REP, HEAD_DIM), q_gain)
    k = rms_norm(qkv[..., dq:dq + dkv].reshape(B, S, GQA_KV_HEADS, HEAD_DIM), k_gain)
    v = qkv[..., dq + dkv:].reshape(B, S, GQA_KV_HEADS, HEAD_DIM)
    q = apply_rope(q, cos, sin)
    k = apply_rope(k, cos, sin)
    kvc = xc @ w_qkv[:, dq:]
    kc = rms_norm(kvc[..., :dkv].reshape(B, L, GQA_KV_HEADS, HEAD_DIM), k_gain)
    vc = kvc[..., dkv:].reshape(B, L, GQA_KV_HEADS, HEAD_DIM)
    k_all = jnp.concatenate([k, kc], axis=1)
    v_all = jnp.concatenate([v, vc], axis=1)
    o = sweep_query_blocks(lambda qb: gqa_attend(qb, k_all, v_all), q)
    yl = o.reshape(B, S, D) @ w_o
    yc = None
    if need_ctx:
        qc = rms_norm((xc @ w_qkv[:, :dq]).reshape(B, L, GQA_KV_HEADS, GQA_REP, HEAD_DIM), q_gain)
        yc = gqa_attend(qc, kc, vc).reshape(B, L, D) @ w_o
    return yl, yc


def diff_mixer(xl, xc, w_qkv, lam, subln_g, w_o, cos, sin, layer_idx, need_ctx):
    B, S, D = xl.shape
    L = xc.shape[1]
    dqk = 2 * DIFF_HEADS * HEAD_DIM
    lam_init = 0.8 - 0.6 * math.exp(-0.3 * layer_idx)
    lamf = lam.astype(jnp.float32)
    lam_full = (jnp.exp(jnp.sum(lamf[0] * lamf[1])) - jnp.exp(jnp.sum(lamf[2] * lamf[3])) + lam_init)

    def attend(q, k, v):
        s = jnp.einsum('bqihd,bkihd->bihqk', q, k).astype(jnp.float32) * (HEAD_DIM ** -0.5)
        p = jax.nn.softmax(s, axis=-1)
        a = (p[:, 0] - lam_full * p[:, 1]).astype(v.dtype)
        o = jnp.einsum('bhqk,bkhe->bqhe', a, v)
        return rms_norm(o, subln_g) * (1 - lam_init)

    qkv = xl @ w_qkv
    q = apply_rope(qkv[..., :dqk].reshape(B, S, 2, DIFF_HEADS, HEAD_DIM), cos, sin)
    k = apply_rope(qkv[..., dqk:2 * dqk].reshape(B, S, 2, DIFF_HEADS, HEAD_DIM), cos, sin)
    v = qkv[..., 2 * dqk:].reshape(B, S, DIFF_HEADS, 2 * HEAD_DIM)
    kvc = xc @ w_qkv[:, dqk:]
    kc = kvc[..., :dqk].reshape(B, L, 2, DIFF_HEADS, HEAD_DIM)
    vc = kvc[..., dqk:].reshape(B, L, DIFF_HEADS, 2 * HEAD_DIM)
    k_all = jnp.concatenate([k, kc], axis=1)
    v_all = jnp.concatenate([v, vc], axis=1)
    o = sweep_query_blocks(lambda qb: attend(qb, k_all, v_all), q)
    yl = o.reshape(B, S, D) @ w_o
    yc = None
    if need_ctx:
        qc = (xc @ w_qkv[:, :dqk]).reshape(B, L, 2, DIFF_HEADS, HEAD_DIM)
        yc = attend(qc, kc, vc).reshape(B, L, D) @ w_o
    return yl, yc


def setup_inputs(seed: int = 0) -> dict:
    key = jax.random.key(seed)
    ks = jax.random.split(key, 20)
    D = D_MODEL
    f32 = jnp.float32

    def w(k, shape, fan_in, mult=1.0):
        return jax.random.normal(k, shape, f32) * (mult * fan_in ** -0.5)

    dq = N_HEADS * HEAD_DIM
    dkv = GQA_KV_HEADS * HEAD_DIM
    return {
        "x": jax.random.normal(ks[0], (BATCH, SEQ, D), f32),
        "c": jax.random.normal(ks[1], (BATCH, D), f32),
        "ctx": jax.random.normal(ks[2], (BATCH, CTX_LEN, D), f32),
        "c_ctx": jax.random.normal(ks[3], (D,), f32),
        "w_mod": w(ks[4], (DEPTH, D, N_MOD * D), D, 0.5),
        "b_mod": 0.01 * jax.random.normal(ks[5], (DEPTH, N_MOD * D), f32),
        "norm_g": 1.0 + 0.05 * jax.random.normal(ks[6], (DEPTH, 6, D), f32),
        "w_ffn_in": w(ks[7], (DEPTH, 2, D, 2 * D_FF), D),
        "w_ffn_out": w(ks[8], (DEPTH, 2, D_FF, D), D_FF),
        "na_w_qkv": w(ks[9], (N_NA, D, 3 * D), D),
        "na_rpb": 0.1 * jax.random.normal(ks[10], (N_NA, N_HEADS, 2 * NA_WIN_H - 1, 2 * NA_WIN_W - 1), f32),
        "na_w_o": w(ks[11], (N_NA, D, D), D),
        "gqa_w_qkv": w(ks[12], (N_GQA, D, dq + 2 * dkv), D),
        "gqa_q_gain": 1.0 + 0.05 * jax.random.normal(ks[13], (N_GQA, HEAD_DIM), f32),
        "gqa_k_gain": 1.0 + 0.05 * jax.random.normal(ks[14], (N_GQA, HEAD_DIM), f32),
        "gqa_w_o": w(ks[15], (N_GQA, D, D), D),
        "diff_w_qkv": w(ks[16], (N_DIFF, D, 3 * D), D),
        "diff_lam": 0.1 * jax.random.normal(ks[17], (N_DIFF, 4, HEAD_DIM), f32),
        "diff_subln_g": 1.0 + 0.05 * jax.random.normal(ks[18], (N_DIFF, 2 * HEAD_DIM), f32),
        "diff_w_o": w(ks[19], (N_DIFF, D, D), D),
    }


def reference(x, c, ctx, c_ctx, w_mod, b_mod, norm_g, w_ffn_in, w_ffn_out,
              na_w_qkv, na_rpb, na_w_o,
              gqa_w_qkv, gqa_q_gain, gqa_k_gain, gqa_w_o,
              diff_w_qkv, diff_lam, diff_subln_g, diff_w_o):
    B, S, D = x.shape
    cos, sin = rope_tables(S, x.dtype)
    sc = jax.nn.silu(c)
    scc = jax.nn.silu(c_ctx)[None]
    h, hc = x, ctx
    for i in range(DEPTH):
        kind = i % N_MIXERS
        j = i // N_MIXERS
        last = i == DEPTH - 1
        mod = jnp.moveaxis((sc @ w_mod[i] + b_mod[i]).reshape(B, N_MOD, 1, D), 1, 0)
        modc = (scc @ w_mod[i] + b_mod[i]).reshape(N_MOD, 1, 1, D)
        g = norm_g[i]
        ffn1 = functools.partial(swiglu, w_in=w_ffn_in[i, 0], w_out=w_ffn_out[i, 0])
        ffn2 = functools.partial(swiglu, w_in=w_ffn_in[i, 1], w_out=w_ffn_out[i, 1])
        h = post_add(h, ffn1(pre(h, g[0], mod[0], mod[1])), g[1], mod[2], 0.5)
        hc = post_add(hc, ffn1(pre(hc, g[0], modc[0], modc[1])), g[1], modc[2], 0.5)
        xl = pre(h, g[2], mod[3], mod[4])
        xc = pre(hc, g[2], modc[3], modc[4])
        if kind == 0:
            yl, yc = natten_mixer(xl, xc, na_w_qkv[j], na_rpb[j], na_w_o[j], not last)
        elif kind == 1:
            yl, yc = gqa_mixer(xl, xc, gqa_w_qkv[j], gqa_q_gain[j], gqa_k_gain[j], gqa_w_o[j],
                               cos, sin, not last)
        else:
            yl, yc = diff_mixer(xl, xc, diff_w_qkv[j], diff_lam[j], diff_subln_g[j], diff_w_o[j],
                                cos, sin, i, not last)
        h = post_add(h, yl, g[3], mod[5], 1.0)
        h = post_add(h, ffn2(pre(h, g[4], mod[6], mod[7])), g[5], mod[8], 0.5)
        if not last:
            hc = post_add(hc, yc, g[3], modc[5], 1.0)
            hc = post_add(hc, ffn2(pre(hc, g[4], modc[6], modc[7])), g[5], modc[8], 0.5)
    return h
```

```python
import functools
import math

import jax
import jax.numpy as jnp
from jax import lax
from jax.experimental import pallas as pl
from jax.experimental.pallas import tpu as pltpu

HEAD_DIM = 64
LANES = 128
GRID_W = 64
NA_WIN_H = 8
NA_WIN_W = 16
ROPE_THETA = 10000.0
EPS = 1e-6
N_MOD = 9
MASK_VALUE = -1e30
VMEM_LIMIT_BYTES = 56 * 1024 * 1024

FFN_CHUNK = 256
PROJ_CHUNK = 256
KV_CHUNK = 512
NA_ROWS_PER_STEP = 8

_NT = (((1,), (1,)), ((), ()))


def _params(*semantics):
    return pltpu.CompilerParams(dimension_semantics=semantics, vmem_limit_bytes=VMEM_LIMIT_BYTES)


def _resident(block_shape, index_map):
    return pl.BlockSpec(block_shape, index_map, pipeline_mode=pl.Buffered(1))


def _rms(x, g):
    return x * lax.rsqrt(jnp.mean(x * x, axis=-1, keepdims=True) + EPS) * g


def _pre(x, g, shift, scale):
    return _rms(x, g) * (1.0 + scale) + shift


def _mod_kernel(c_ref, w_ref, b_ref, o_ref):
    c = c_ref[...]
    sc = (c * jax.nn.sigmoid(c)).astype(jnp.bfloat16)
    w = w_ref[0].astype(jnp.bfloat16)
    o_ref[0] = jnp.dot(sc, w, preferred_element_type=jnp.float32) + b_ref[0]


def _modulation(cond, w_mod, b_mod):
    depth, d, n = w_mod.shape
    r = cond.shape[0]
    tn = n // N_MOD
    return pl.pallas_call(
        _mod_kernel,
        grid=(depth, n // tn),
        in_specs=[
            pl.BlockSpec((r, d), lambda i, j: (0, 0)),
            pl.BlockSpec((1, d, tn), lambda i, j: (i, 0, j)),
            pl.BlockSpec((1, 1, tn), lambda i, j: (i, 0, j)),
        ],
        out_specs=pl.BlockSpec((1, r, tn), lambda i, j: (i, 0, j)),
        out_shape=jax.ShapeDtypeStruct((depth, r, n), jnp.float32),
        compiler_params=_params("arbitrary", "arbitrary"),
        name="modulation",
    )(cond, w_mod, b_mod.reshape(depth, 1, n))


def _ffn_kernel(x_ref, mod_ref, g_ref, win_ref, wout_ref, o_ref, act_ref, *, mod_row, g_row):
    d_ff = wout_ref.shape[0]
    x = x_ref[0]
    shift = mod_ref[0, mod_row:mod_row + 1, :]
    scale = mod_ref[0, mod_row + 1:mod_row + 2, :]
    gate = mod_ref[0, mod_row + 2:mod_row + 3, :]
    xm = _pre(x, g_ref[g_row:g_row + 1, :], shift, scale).astype(jnp.bfloat16)
    for j in range(d_ff // FFN_CHUNK):
        lo = j * FFN_CHUNK
        a = jnp.dot(xm, win_ref[:, lo:lo + FFN_CHUNK], preferred_element_type=jnp.float32)
        b = jnp.dot(xm, win_ref[:, d_ff + lo:d_ff + lo + FFN_CHUNK], preferred_element_type=jnp.float32)
        act_ref[:, lo:lo + FFN_CHUNK] = (a * jax.nn.sigmoid(a) * b).astype(jnp.bfloat16)
    y = jnp.dot(act_ref[...], wout_ref[...], preferred_element_type=jnp.float32)
    o_ref[0] = x + 0.5 * gate * _rms(y, g_ref[g_row + 1:g_row + 2, :])


def _mod_spec(d, cond_row):
    if cond_row is None:
        return pl.BlockSpec((1, N_MOD, d), lambda b, i: (b, 0, 0))
    return pl.BlockSpec((1, N_MOD, d), lambda b, i: (cond_row, 0, 0))


def _ffn(h, mod, cond_row, g, win, wout, *, which, tm):
    bsz, t, d = h.shape
    d_ff = wout.shape[0]
    kern = functools.partial(_ffn_kernel, mod_row=6 * which, g_row=4 * which)
    return pl.pallas_call(
        kern,
        grid=(bsz, t // tm),
        in_specs=[
            pl.BlockSpec((1, tm, d), lambda b, i: (b, i, 0)),
            _mod_spec(d, cond_row),
            _resident(g.shape, lambda b, i: (0, 0)),
            _resident(win.shape, lambda b, i: (0, 0)),
            _resident(wout.shape, lambda b, i: (0, 0)),
        ],
        out_specs=pl.BlockSpec((1, tm, d), lambda b, i: (b, i, 0)),
        out_shape=jax.ShapeDtypeStruct(h.shape, h.dtype),
        scratch_shapes=[pltpu.VMEM((tm, d_ff), jnp.bfloat16)],
        compiler_params=_params("parallel", "parallel"),
        name=f"ffn{which}",
    )(h, mod, g, win, wout)


def _head_rms(y, gain):
    lane = lax.broadcasted_iota(jnp.int32, y.shape, 1)
    low = lane < HEAD_DIM
    y2 = y * y
    ss_low = jnp.sum(jnp.where(low, y2, 0.0), axis=-1, keepdims=True)
    ss_high = jnp.sum(jnp.where(low, 0.0, y2), axis=-1, keepdims=True)
    r = jnp.where(low, lax.rsqrt(ss_low / HEAD_DIM + EPS), lax.rsqrt(ss_high / HEAD_DIM + EPS))
    return y * r * gain


def _rotary(y, cos, sin_up, sin_down):
    up = pltpu.roll(y, LANES - HEAD_DIM // 4, axis=1)
    down = pltpu.roll(y, HEAD_DIM // 4, axis=1)
    return y * cos + up * sin_up + down * sin_down


def _proj_kernel(x_ref, mod_ref, g_ref, w_ref, gain_ref, cos_ref, sup_ref, sdn_ref, o_ref, *, epilogues):
    x = x_ref[0]
    shift = mod_ref[0, 3:4, :]
    scale = mod_ref[0, 4:5, :]
    xm = _pre(x, g_ref[2:3, :], shift, scale).astype(jnp.bfloat16)
    n_blocks = len(epilogues)
    per_dot = PROJ_CHUNK // LANES
    for c in range(0, n_blocks, per_dot):
        width = min(per_dot, n_blocks - c) * LANES
        y_all = jnp.dot(xm, w_ref[:, c * LANES:c * LANES + width], preferred_element_type=jnp.float32)
        for s in range(width // LANES):
            norm, rope, qscale = epilogues[c + s]
            y = y_all[:, s * LANES:(s + 1) * LANES]
            if norm is not None:
                y = _head_rms(y, gain_ref[norm:norm + 1, :])
            if rope:
                y = _rotary(y, cos_ref[...], sup_ref[...], sdn_ref[...])
            if qscale:
                y = y * (HEAD_DIM ** -0.5)
            o_ref[0, :, (c + s) * LANES:(c + s + 1) * LANES] = y.astype(o_ref.dtype)


def _project(h, mod, cond_row, g, w, gains, rope, epilogues, *, tm):
    bsz, t, d = h.shape
    n = w.shape[1]
    cos, sin_up, sin_down = rope
    rope_spec = pl.BlockSpec((tm, LANES), lambda b, i: (i, 0))
    kern = functools.partial(_proj_kernel, epilogues=tuple(epilogues))
    return pl.pallas_call(
        kern,
        grid=(bsz, t // tm),
        in_specs=[
            pl.BlockSpec((1, tm, d), lambda b, i: (b, i, 0)),
            _mod_spec(d, cond_row),
            _resident(g.shape, lambda b, i: (0, 0)),
            _resident(w.shape, lambda b, i: (0, 0)),
            _resident(gains.shape, lambda b, i: (0, 0)),
            rope_spec, rope_spec, rope_spec,
        ],
        out_specs=pl.BlockSpec((1, tm, n), lambda b, i: (b, i, 0)),
        out_shape=jax.ShapeDtypeStruct((bsz, t, n), jnp.bfloat16),
        compiler_params=_params("parallel", "parallel"),
        name="qkv_proj",
    )(h, mod, g, w, gains, cos, sin_up, sin_down)


def _oproj_kernel(h_ref, o_ref, mod_ref, g_ref, w_ref, out_ref):
    y = jnp.dot(o_ref[0], w_ref[...], preferred_element_type=jnp.float32)
    out_ref[0] = h_ref[0] + mod_ref[0, 5:6, :] * _rms(y, g_ref[3:4, :])


def _oproj(h, o, mod, cond_row, g, w, *, tm):
    bsz, t, d = h.shape
    return pl.pallas_call(
        _oproj_kernel,
        grid=(bsz, t // tm),
        in_specs=[
            pl.BlockSpec((1, tm, d), lambda b, i: (b, i, 0)),
            pl.BlockSpec((1, tm, d), lambda b, i: (b, i, 0)),
            _mod_spec(d, cond_row),
            _resident(g.shape, lambda b, i: (0, 0)),
            _resident(w.shape, lambda b, i: (0, 0)),
        ],
        out_specs=pl.BlockSpec((1, tm, d), lambda b, i: (b, i, 0)),
        out_shape=jax.ShapeDtypeStruct(h.shape, h.dtype),
        compiler_params=_params("parallel", "parallel"),
        name="out_proj",
    )(h, o, mod, g, w)


def _split_pair(q):
    lane = lax.broadcasted_iota(jnp.int32, q.shape, 1)
    zero = jnp.zeros_like(q)
    return jnp.concatenate([jnp.where(lane < HEAD_DIM, q, zero), jnp.where(lane < HEAD_DIM, zero, q)], axis=0)


def _merge_pair(o):
    tq = o.shape[0] // 2
    lane = lax.broadcasted_iota(jnp.int32, (tq, LANES), 1)
    return jnp.where(lane < HEAD_DIM, o[:tq], o[tq:])


def _online_step(qs, k, m_ref, l_ref):
    s = lax.dot_general(qs, k, _NT, preferred_element_type=jnp.float32)
    m_prev = m_ref[...]
    m_new = jnp.maximum(m_prev, jnp.max(s, axis=-1, keepdims=True))
    alpha = jnp.exp(m_prev - m_new)
    p = jnp.exp(s - m_new)
    l_ref[...] = alpha * l_ref[...] + jnp.sum(p, axis=-1, keepdims=True)
    m_ref[...] = m_new
    return alpha, p.astype(jnp.bfloat16)


def _for_each_chunk(length, fn):
    if length <= KV_CHUNK:
        fn(0, length)
        return
    assert length % KV_CHUNK == 0

    def body(j, carry):
        fn(pl.multiple_of(j * KV_CHUNK, KV_CHUNK), KV_CHUNK)
        return carry

    lax.fori_loop(0, length // KV_CHUNK, body, 0)


def _pair_attn_kernel(*refs, n_src):
    q_ref = refs[0]
    kv_refs = refs[1:1 + 2 * n_src]
    o_ref = refs[1 + 2 * n_src]
    m_ref, l_ref, acc_ref = refs[2 + 2 * n_src:]
    qs = _split_pair(q_ref[0])
    m_ref[...] = jnp.full(m_ref.shape, -jnp.inf, jnp.float32)
    l_ref[...] = jnp.zeros(l_ref.shape, jnp.float32)
    acc_ref[...] = jnp.zeros(acc_ref.shape, jnp.float32)
    for s in range(n_src):
        k_ref, v_ref = kv_refs[2 * s], kv_refs[2 * s + 1]

        def step(start, size, k_ref=k_ref, v_ref=v_ref):
            alpha, p = _online_step(qs, k_ref[0, pl.ds(start, size), :], m_ref, l_ref)
            pv = jnp.dot(p, v_ref[0, pl.ds(start, size), :], preferred_element_type=jnp.float32)
            acc_ref[...] = alpha * acc_ref[...] + pv

        _for_each_chunk(k_ref.shape[1], step)
    o_ref[0] = _merge_pair(acc_ref[...] / l_ref[...]).astype(o_ref.dtype)


def _pair_attention(q_arr, q_block0, sources, n_pairs, k_of_pair, *, tq):
    bsz, t, _ = q_arr.shape
    in_specs = [pl.BlockSpec((1, tq, LANES), lambda b, p, i: (b, i, q_block0 + p))]
    args = [q_arr]
    for arr, kb, vb in sources:
        tk = arr.shape[1]
        in_specs.append(pl.BlockSpec((1, tk, LANES), lambda b, p, i, kb=kb: (b, 0, kb + k_of_pair(p))))
        in_specs.append(pl.BlockSpec((1, tk, LANES), lambda b, p, i, vb=vb: (b, 0, vb + k_of_pair(p))))
        args += [arr, arr]
    return pl.pallas_call(
        functools.partial(_pair_attn_kernel, n_src=len(sources)),
        grid=(bsz, n_pairs, t // tq),
        in_specs=in_specs,
        out_specs=pl.BlockSpec((1, tq, LANES), lambda b, p, i: (b, i, p)),
        out_shape=jax.ShapeDtypeStruct((bsz, t, n_pairs * LANES), jnp.bfloat16),
        scratch_shapes=[
            pltpu.VMEM((2 * tq, 1), jnp.float32),
            pltpu.VMEM((2 * tq, 1), jnp.float32),
            pltpu.VMEM((2 * tq, LANES), jnp.float32),
        ],
        compiler_params=_params("parallel", "parallel", "arbitrary"),
        name="pair_attention",
    )(*args)


def _diff_attn_kernel(*refs, n_src, lam_init):
    q1_ref, q2_ref, lam_ref, g_ref = refs[:4]
    kv_refs = refs[4:4 + 4 * n_src]
    o_ref = refs[4 + 4 * n_src]
    m_refs = refs[5 + 4 * n_src:7 + 4 * n_src]
    l_refs = refs[7 + 4 * n_src:9 + 4 * n_src]
    acc_refs = refs[9 + 4 * n_src:11 + 4 * n_src]
    tq = q1_ref.shape[1]
    qs = (_split_pair(q1_ref[0]), _split_pair(q2_ref[0]))
    for i in range(2):
        m_refs[i][...] = jnp.full(m_refs[i].shape, -jnp.inf, jnp.float32)
        l_refs[i][...] = jnp.zeros(l_refs[i].shape, jnp.float32)
        acc_refs[i][...] = jnp.zeros(acc_refs[i].shape, jnp.float32)
    for s in range(n_src):
        k1_ref, k2_ref, vlo_ref, vhi_ref = kv_refs[4 * s:4 * s + 4]

        def step(start, size, k_refs=(k1_ref, k2_ref), vlo_ref=vlo_ref, vhi_ref=vhi_ref):
            v_lo = vlo_ref[0, pl.ds(start, size), :]
            v_hi = vhi_ref[0, pl.ds(start, size), :]
            for i in range(2):
                alpha, p = _online_step(qs[i], k_refs[i][0, pl.ds(start, size), :], m_refs[i], l_refs[i])
                pv = jnp.concatenate([
                    jnp.dot(p[:tq], v_lo, preferred_element_type=jnp.float32),
                    jnp.dot(p[tq:], v_hi, preferred_element_type=jnp.float32)], axis=0)
                acc_refs[i][...] = alpha * acc_refs[i][...] + pv

        _for_each_chunk(k1_ref.shape[1], step)
    lam = lam_ref[...]
    lam_full = (jnp.exp(jnp.sum(lam[0:1] * lam[1:2], axis=-1, keepdims=True))
                - jnp.exp(jnp.sum(lam[2:3] * lam[3:4], axis=-1, keepdims=True)) + lam_init)
    o = acc_refs[0][...] / l_refs[0][...] - lam_full * (acc_refs[1][...] / l_refs[1][...])
    o = _rms(o, g_ref[...]) * (1.0 - lam_init)
    o_ref[0, :, :LANES] = o[:tq].astype(o_ref.dtype)
    o_ref[0, :, LANES:] = o[tq:].astype(o_ref.dtype)


def _diff_attention(q_arr, sources, lam, subln_g, lam_init, *, tq):
    bsz, t, _ = q_arr.shape
    n_pairs = 4
    in_specs = [
        pl.BlockSpec((1, tq, LANES), lambda b, p, i: (b, i, p)),
        pl.BlockSpec((1, tq, LANES), lambda b, p, i: (b, i, n_pairs + p)),
        pl.BlockSpec(lam.shape, lambda b, p, i: (0, 0)),
        pl.BlockSpec(subln_g.shape, lambda b, p, i: (0, 0)),
    ]
    args = [q_arr, q_arr, lam, subln_g]
    for arr in sources:
        tk = arr.shape[1]
        in_specs += [
            pl.BlockSpec((1, tk, LANES), lambda b, p, i: (b, 0, 8 + p)),
            pl.BlockSpec((1, tk, LANES), lambda b, p, i: (b, 0, 12 + p)),
            pl.BlockSpec((1, tk, LANES), lambda b, p, i: (b, 0, 16 + 2 * p)),
            pl.BlockSpec((1, tk, LANES), lambda b, p, i: (b, 0, 17 + 2 * p)),
        ]
        args += [arr] * 4
    stat = pltpu.VMEM((2 * tq, 1), jnp.float32)
    acc = pltpu.VMEM((2 * tq, LANES), jnp.float32)
    return pl.pallas_call(
        functools.partial(_diff_attn_kernel, n_src=len(sources), lam_init=lam_init),
        grid=(bsz, n_pairs, t // tq),
        in_specs=in_specs,
        out_specs=pl.BlockSpec((1, tq, 2 * LANES), lambda b, p, i: (b, i, p)),
        out_shape=jax.ShapeDtypeStruct((bsz, t, 2 * LANES * n_pairs), jnp.bfloat16),
        scratch_shapes=[stat, stat, stat, stat, acc, acc],
        compiler_params=_params("parallel", "parallel", "arbitrary"),
        name="diff_attention",
    )(*args)


def _na_attn_kernel(q_ref, k_ref, v_ref, kc_ref, vc_ref, bias_ref, o_ref, *, n_rows):
    step = pl.program_id(2)
    kc = kc_ref[0]
    vc = vc_ref[0]
    n_nb = NA_WIN_H * GRID_W

    def row(rr, carry):
        r = step * NA_ROWS_PER_STEP + rr
        rs = jnp.clip(r - NA_WIN_H // 2, 0, n_rows - NA_WIN_H)
        tok0 = pl.multiple_of(rs * GRID_W, GRID_W)
        q0 = pl.multiple_of(rr * GRID_W, GRID_W)
        qs = _split_pair(q_ref[0, pl.ds(q0, GRID_W), :])
        k_nb = k_ref[0, pl.ds(tok0, n_nb), :]
        v_nb = v_ref[0, pl.ds(tok0, n_nb), :]
        rel0 = rs - r + NA_WIN_H - 1
        bias = jnp.concatenate([bias_ref[0, rel0 + 2 * c] for c in range(NA_WIN_H // 2)], axis=1)
        s_nb = lax.dot_general(qs, k_nb, _NT, preferred_element_type=jnp.float32) + bias
        s_cx = lax.dot_general(qs, kc, _NT, preferred_element_type=jnp.float32)
        m = jnp.maximum(jnp.max(s_nb, axis=-1, keepdims=True), jnp.max(s_cx, axis=-1, keepdims=True))
        p_nb = jnp.exp(s_nb - m)
        p_cx = jnp.exp(s_cx - m)
        l = jnp.sum(p_nb, axis=-1, keepdims=True) + jnp.sum(p_cx, axis=-1, keepdims=True)
        o = (jnp.dot(p_nb.astype(jnp.bfloat16), v_nb, preferred_element_type=jnp.float32)
             + jnp.dot(p_cx.astype(jnp.bfloat16), vc, preferred_element_type=jnp.float32))
        o_ref[0, pl.ds(q0, GRID_W), :] = _merge_pair(o / l).astype(o_ref.dtype)
        return carry

    lax.fori_loop(0, NA_ROWS_PER_STEP, row, 0)


def _na_attention(qkv, qkv_ctx, bias):
    bsz, s, _ = qkv.shape
    l_ctx = qkv_ctx.shape[1]
    n_pairs = 8
    tq = NA_ROWS_PER_STEP * GRID_W
    return pl.pallas_call(
        functools.partial(_na_attn_kernel, n_rows=s // GRID_W),
        grid=(bsz, n_pairs, s // tq),
        in_specs=[
            pl.BlockSpec((1, tq, LANES), lambda b, p, i: (b, i, p)),
            pl.BlockSpec((1, s, LANES), lambda b, p, i: (b, 0, 8 + p)),
            pl.BlockSpec((1, s, LANES), lambda b, p, i: (b, 0, 16 + p)),
            pl.BlockSpec((1, l_ctx, LANES), lambda b, p, i: (b, 0, 8 + p)),
            pl.BlockSpec((1, l_ctx, LANES), lambda b, p, i: (b, 0, 16 + p)),
            pl.BlockSpec((1,) + bias.shape[1:], lambda b, p, i: (p, 0, 0, 0)),
        ],
        out_specs=pl.BlockSpec((1, tq, LANES), lambda b, p, i: (b, i, p)),
        out_shape=jax.ShapeDtypeStruct((bsz, s, n_pairs * LANES), jnp.bfloat16),
        compiler_params=_params("parallel", "parallel", "arbitrary"),
        name="na_attention",
    )(qkv, qkv, qkv, qkv_ctx, qkv_ctx, bias)


def _rope_tables(n):
    t = jnp.arange(n, dtype=jnp.int32)
    row = (t // GRID_W).astype(jnp.float32)
    col = (t % GRID_W).astype(jnp.float32)
    nf = HEAD_DIM // 4
    inv = 1.0 / (ROPE_THETA ** (jnp.arange(nf, dtype=jnp.float32) / nf))
    ar = row[:, None] * inv
    ac = col[:, None] * inv
    zero = jnp.zeros_like(ar)
    cos = jnp.concatenate([jnp.cos(ar), jnp.cos(ar), jnp.cos(ac), jnp.cos(ac)], axis=-1)
    sin_up = jnp.concatenate([-jnp.sin(ar), zero, -jnp.sin(ac), zero], axis=-1)
    sin_down = jnp.concatenate([zero, jnp.sin(ar), zero, jnp.sin(ac)], axis=-1)
    return tuple(jnp.tile(x, (1, LANES // HEAD_DIM)) for x in (cos, sin_up, sin_down))


def _na_bias_table(rpb):
    cpos = jnp.arange(GRID_W, dtype=jnp.int32)
    cidx = jnp.clip(cpos[None, :] - cpos[:, None] + NA_WIN_W - 1, 0, 2 * NA_WIN_W - 2)
    cstart = jnp.clip(cpos - NA_WIN_W // 2, 0, GRID_W - NA_WIN_W)
    col_ok = (cpos[None, :] >= cstart[:, None]) & (cpos[None, :] < cstart[:, None] + NA_WIN_W)
    t = jnp.where(col_ok, rpb[:, :, cidx], MASK_VALUE)
    t2 = jnp.concatenate([t[:, :-1], t[:, 1:]], axis=-1)
    n_heads, n_rel = t2.shape[:2]
    t2 = t2.reshape(n_heads // 2, 2, n_rel, GRID_W, LANES).transpose(0, 2, 1, 3, 4)
    return t2.reshape(n_heads // 2, n_rel, 2 * GRID_W, LANES)


def _dup_heads(w):
    d, n = w.shape
    return jnp.tile(w.reshape(d, n // HEAD_DIM, 1, HEAD_DIM), (1, 1, 2, 1)).reshape(d, 2 * n)


def _tile_gain(g):
    return jnp.tile(g, LANES // HEAD_DIM)


PLAIN = (None, False, False)


def kernel(x, c, ctx, c_ctx, w_mod, b_mod, norm_g, w_ffn_in, w_ffn_out, na_w_qkv, na_rpb, na_w_o, gqa_w_qkv, gqa_q_gain, gqa_k_gain, gqa_w_o, diff_w_qkv, diff_lam, diff_subln_g, diff_w_o):
    bsz, seq, d = x.shape
    l_ctx = ctx.shape[1]
    depth = w_mod.shape[0]
    bf16 = jnp.bfloat16
    tm = 512
    tm_ctx = l_ctx
    tq = 256

    n_cond = 8 * ((bsz + 1 + 7) // 8)
    cond = jnp.concatenate([c, c_ctx[None], jnp.zeros((n_cond - bsz - 1, d), c.dtype)], axis=0)
    mod_all = _modulation(cond, w_mod, b_mod).reshape(depth, n_cond, N_MOD, d)
    ctx_row = bsz

    rope = _rope_tables(seq)
    no_rope = tuple(jnp.zeros((l_ctx, LANES), jnp.float32) for _ in range(3))
    no_gain = jnp.ones((2, LANES), jnp.float32)

    h, hc = x, ctx
    for i in range(depth):
        kind, j, last = i % 3, i // 3, i == depth - 1
        mod, g = mod_all[i], norm_g[i]
        win = w_ffn_in[i].astype(bf16)
        wout = w_ffn_out[i].astype(bf16)

        h = _ffn(h, mod, None, g, win[0], wout[0], which=0, tm=tm)
        hc = _ffn(hc, mod, ctx_row, g, win[0], wout[0], which=0, tm=tm_ctx)

        if kind == 0:
            w = na_w_qkv[j].astype(bf16)
            ep = [(None, False, True)] * 8 + [PLAIN] * 16
            qkv = _project(h, mod, None, g, w, no_gain, rope, ep, tm=tm)
            qkv_c = _project(hc, mod, ctx_row, g, w, no_gain, no_rope, ep, tm=tm_ctx)
            o = _na_attention(qkv, qkv_c, _na_bias_table(na_rpb[j]))
            if not last:
                oc = _pair_attention(qkv_c, 0, [(qkv_c, 8, 16)], 8, lambda p: p, tq=l_ctx)
            w_o = na_w_o[j].astype(bf16)
        elif kind == 1:
            wf = gqa_w_qkv[j]
            dq = d
            dkv = (wf.shape[1] - dq) // 2
            w = jnp.concatenate([wf[:, :dq], _dup_heads(wf[:, dq:dq + dkv]), _dup_heads(wf[:, dq + dkv:])],
                                axis=1).astype(bf16)
            gains = jnp.stack([_tile_gain(gqa_q_gain[j]), _tile_gain(gqa_k_gain[j])])
            n_kv = dkv // HEAD_DIM
            ep = [(0, True, True)] * 8 + [(1, True, False)] * n_kv + [PLAIN] * n_kv
            ep_c = [(0, False, True)] * 8 + [(1, False, False)] * n_kv + [PLAIN] * n_kv
            qkv = _project(h, mod, None, g, w, gains, rope, ep, tm=tm)
            qkv_c = _project(hc, mod, ctx_row, g, w, gains, no_rope, ep_c, tm=tm_ctx)
            o = _pair_attention(qkv, 0, [(qkv, 8, 8 + n_kv), (qkv_c, 8, 8 + n_kv)], 8, lambda p: p // 2, tq=tq)
            if not last:
                oc = _pair_attention(qkv_c, 0, [(qkv_c, 8, 8 + n_kv)], 8, lambda p: p // 2, tq=l_ctx)
            w_o = gqa_w_o[j].astype(bf16)
        else:
            w = diff_w_qkv[j].astype(bf16)
            lam_init = 0.8 - 0.6 * math.exp(-0.3 * i)
            ep = [(None, True, True)] * 8 + [(None, True, False)] * 8 + [PLAIN] * 8
            ep_c = [(None, False, True)] * 8 + [PLAIN] * 16
            qkv = _project(h, mod, None, g, w, no_gain, rope, ep, tm=tm)
            qkv_c = _project(hc, mod, ctx_row, g, w, no_gain, no_rope, ep_c, tm=tm_ctx)
            sub_g = diff_subln_g[j][None]
            o = _diff_attention(qkv, [qkv, qkv_c], diff_lam[j], sub_g, lam_init, tq=tq)
            if not last:
                oc = _diff_attention(qkv_c, [qkv_c], diff_lam[j], sub_g, lam_init, tq=l_ctx)
            w_o = diff_w_o[j].astype(bf16)

        h = _oproj(h, o, mod, None, g, w_o, tm=tm)
        h = _ffn(h, mod, None, g, win[1], wout[1], which=1, tm=tm)
        if not last:
            hc = _oproj(hc, oc, mod, ctx_row, g, w_o, tm=tm_ctx)
            hc = _ffn(hc, mod, ctx_row, g, win[1], wout[1], which=1, tm=tm_ctx)
    return h
```

```python
import functools
import math
from typing import NamedTuple, Optional

import jax
import jax.numpy as jnp
from jax import lax
from jax.experimental import pallas as pl
from jax.experimental.pallas import tpu as pltpu

HEAD_DIM = 64
LANES = 128
GRID_W = 64
NA_WIN_H = 8
NA_WIN_W = 16
ROPE_THETA = 10000.0
EPS = 1e-6
N_MOD = 9
MASK_VALUE = -1e30
VMEM_LIMIT_BYTES = 56 * 1024 * 1024

FFN_CHUNK = 256
PROJ_CHUNK = 256
KV_CHUNK = 512
NA_ROWS_PER_STEP = 8
NA_GROUP_ROWS = 2

_NT = (((1,), (1,)), ((), ()))


def _params(*semantics):
    return pltpu.CompilerParams(dimension_semantics=semantics, vmem_limit_bytes=VMEM_LIMIT_BYTES)


def _resident(block_shape, index_map):
    return pl.BlockSpec(block_shape, index_map, pipeline_mode=pl.Buffered(1))


def _rms(x, g):
    return x * lax.rsqrt(jnp.mean(x * x, axis=-1, keepdims=True) + EPS) * g


def _pre(x, g, shift, scale):
    return _rms(x, g) * (1.0 + scale) + shift


def _mod_kernel(c_ref, w_ref, b_ref, o_ref):
    c = c_ref[...]
    sc = (c * jax.nn.sigmoid(c)).astype(jnp.bfloat16)
    w = w_ref[0].astype(jnp.bfloat16)
    o_ref[0] = jnp.dot(sc, w, preferred_element_type=jnp.float32) + b_ref[0]


def _modulation(cond, w_mod, b_mod):
    depth, d, n = w_mod.shape
    r = cond.shape[0]
    tn = n // N_MOD
    return pl.pallas_call(
        _mod_kernel,
        grid=(depth, n // tn),
        in_specs=[
            pl.BlockSpec((r, d), lambda i, j: (0, 0)),
            pl.BlockSpec((1, d, tn), lambda i, j: (i, 0, j)),
            pl.BlockSpec((1, 1, tn), lambda i, j: (i, 0, j)),
        ],
        out_specs=pl.BlockSpec((1, r, tn), lambda i, j: (i, 0, j)),
        out_shape=jax.ShapeDtypeStruct((depth, r, n), jnp.float32),
        compiler_params=_params("arbitrary", "arbitrary"),
        name="modulation",
    )(cond, w_mod, b_mod.reshape(depth, 1, n))


def _ffn_kernel(x_ref, mod_ref, g_ref, win_ref, wout_ref, o_ref, act_ref, *, mod_row, g_row):
    d_ff = wout_ref.shape[0]
    x = x_ref[0]
    shift = mod_ref[0, mod_row:mod_row + 1, :]
    scale = mod_ref[0, mod_row + 1:mod_row + 2, :]
    gate = mod_ref[0, mod_row + 2:mod_row + 3, :]
    xm = _pre(x, g_ref[g_row:g_row + 1, :], shift, scale).astype(jnp.bfloat16)
    for j in range(d_ff // FFN_CHUNK):
        lo = j * FFN_CHUNK
        a = jnp.dot(xm, win_ref[:, lo:lo + FFN_CHUNK], preferred_element_type=jnp.float32)
        b = jnp.dot(xm, win_ref[:, d_ff + lo:d_ff + lo + FFN_CHUNK], preferred_element_type=jnp.float32)
        act_ref[:, lo:lo + FFN_CHUNK] = (a * jax.nn.sigmoid(a) * b).astype(jnp.bfloat16)
    y = jnp.dot(act_ref[...], wout_ref[...], preferred_element_type=jnp.float32)
    o_ref[0] = x + 0.5 * gate * _rms(y, g_ref[g_row + 1:g_row + 2, :])


def _mod_spec(d, cond_row):
    if cond_row is None:
        return pl.BlockSpec((1, N_MOD, d), lambda b, i: (b, 0, 0))
    return pl.BlockSpec((1, N_MOD, d), lambda b, i: (cond_row, 0, 0))


def _ffn(h, mod, cond_row, g, win, wout, *, which, tm):
    bsz, t, d = h.shape
    d_ff = wout.shape[0]
    kern = functools.partial(_ffn_kernel, mod_row=6 * which, g_row=4 * which)
    return pl.pallas_call(
        kern,
        grid=(bsz, t // tm),
        in_specs=[
            pl.BlockSpec((1, tm, d), lambda b, i: (b, i, 0)),
            _mod_spec(d, cond_row),
            _resident(g.shape, lambda b, i: (0, 0)),
            _resident(win.shape, lambda b, i: (0, 0)),
            _resident(wout.shape, lambda b, i: (0, 0)),
        ],
        out_specs=pl.BlockSpec((1, tm, d), lambda b, i: (b, i, 0)),
        out_shape=jax.ShapeDtypeStruct(h.shape, h.dtype),
        scratch_shapes=[pltpu.VMEM((tm, d_ff), jnp.bfloat16)],
        compiler_params=_params("parallel", "parallel"),
        name=f"ffn{which}",
    )(h, mod, g, win, wout)


class _Ep(NamedTuple):
    norm: Optional[int] = None
    rope: bool = False
    qscale: bool = False
    token_major: bool = True
    feature_major: bool = False


def _head_rms(y, gain):
    lane = lax.broadcasted_iota(jnp.int32, y.shape, 1)
    low = lane < HEAD_DIM
    y2 = y * y
    ss_low = jnp.sum(jnp.where(low, y2, 0.0), axis=-1, keepdims=True)
    ss_high = jnp.sum(jnp.where(low, 0.0, y2), axis=-1, keepdims=True)
    r = jnp.where(low, lax.rsqrt(ss_low / HEAD_DIM + EPS), lax.rsqrt(ss_high / HEAD_DIM + EPS))
    return y * r * gain


def _rotary(y, cos, sin_up, sin_down):
    up = pltpu.roll(y, LANES - HEAD_DIM // 4, axis=1)
    down = pltpu.roll(y, HEAD_DIM // 4, axis=1)
    return y * cos + up * sin_up + down * sin_down


def _proj_kernel(x_ref, mod_ref, g_ref, w_ref, gain_ref, cos_ref, sup_ref, sdn_ref, *out_refs, epilogues):
    o_ref = out_refs[0]
    x = x_ref[0]
    shift = mod_ref[0, 3:4, :]
    scale = mod_ref[0, 4:5, :]
    xm = _pre(x, g_ref[2:3, :], shift, scale).astype(jnp.bfloat16)
    n_blocks = len(epilogues)
    per_dot = PROJ_CHUNK // LANES
    main_at = 0
    feat_at = 0
    for c in range(0, n_blocks, per_dot):
        width = min(per_dot, n_blocks - c) * LANES
        y_all = jnp.dot(xm, w_ref[:, c * LANES:c * LANES + width], preferred_element_type=jnp.float32)
        for s in range(width // LANES):
            ep = epilogues[c + s]
            y = y_all[:, s * LANES:(s + 1) * LANES]
            if ep.norm is not None:
                y = _head_rms(y, gain_ref[ep.norm:ep.norm + 1, :])
            if ep.rope:
                y = _rotary(y, cos_ref[...], sup_ref[...], sdn_ref[...])
            if ep.qscale:
                y = y * (HEAD_DIM ** -0.5)
            if ep.token_major:
                o_ref[0, :, main_at * LANES:(main_at + 1) * LANES] = y.astype(o_ref.dtype)
                main_at += 1
            if ep.feature_major:
                out_refs[1][0, feat_at * LANES:(feat_at + 1) * LANES, :] = y.T.astype(o_ref.dtype)
                feat_at += 1


def _project(h, mod, cond_row, g, w, gains, rope, epilogues, *, tm):
    bsz, t, d = h.shape
    n_main = sum(ep.token_major for ep in epilogues)
    n_feat = sum(ep.feature_major for ep in epilogues)
    cos, sin_up, sin_down = rope
    rope_spec = pl.BlockSpec((tm, LANES), lambda b, i: (i, 0))
    out_specs = [pl.BlockSpec((1, tm, n_main * LANES), lambda b, i: (b, i, 0))]
    out_shape = [jax.ShapeDtypeStruct((bsz, t, n_main * LANES), jnp.bfloat16)]
    if n_feat:
        out_specs.append(pl.BlockSpec((1, n_feat * LANES, tm), lambda b, i: (b, 0, i)))
        out_shape.append(jax.ShapeDtypeStruct((bsz, n_feat * LANES, t), jnp.bfloat16))
    outs = pl.pallas_call(
        functools.partial(_proj_kernel, epilogues=tuple(epilogues)),
        grid=(bsz, t // tm),
        in_specs=[
            pl.BlockSpec((1, tm, d), lambda b, i: (b, i, 0)),
            _mod_spec(d, cond_row),
            _resident(g.shape, lambda b, i: (0, 0)),
            _resident(w.shape, lambda b, i: (0, 0)),
            _resident(gains.shape, lambda b, i: (0, 0)),
            rope_spec, rope_spec, rope_spec,
        ],
        out_specs=out_specs,
        out_shape=out_shape,
        compiler_params=_params("parallel", "parallel"),
        name="qkv_proj",
    )(h, mod, g, w, gains, cos, sin_up, sin_down)
    return outs if n_feat else (outs[0], None)


def _oproj_kernel(h_ref, o_ref, mod_ref, g_ref, w_ref, out_ref):
    y = jnp.dot(o_ref[0], w_ref[...], preferred_element_type=jnp.float32)
    out_ref[0] = h_ref[0] + mod_ref[0, 5:6, :] * _rms(y, g_ref[3:4, :])


def _oproj(h, o, mod, cond_row, g, w, *, tm):
    bsz, t, d = h.shape
    return pl.pallas_call(
        _oproj_kernel,
        grid=(bsz, t // tm),
        in_specs=[
            pl.BlockSpec((1, tm, d), lambda b, i: (b, i, 0)),
            pl.BlockSpec((1, tm, d), lambda b, i: (b, i, 0)),
            _mod_spec(d, cond_row),
            _resident(g.shape, lambda b, i: (0, 0)),
            _resident(w.shape, lambda b, i: (0, 0)),
        ],
        out_specs=pl.BlockSpec((1, tm, d), lambda b, i: (b, i, 0)),
        out_shape=jax.ShapeDtypeStruct(h.shape, h.dtype),
        compiler_params=_params("parallel", "parallel"),
        name="out_proj",
    )(h, o, mod, g, w)


def _split_pair(q):
    lane = lax.broadcasted_iota(jnp.int32, q.shape, 1)
    zero = jnp.zeros_like(q)
    return jnp.concatenate([jnp.where(lane < HEAD_DIM, q, zero), jnp.where(lane < HEAD_DIM, zero, q)], axis=0)


def _merge_pair(o):
    tq = o.shape[0] // 2
    lane = lax.broadcasted_iota(jnp.int32, (tq, LANES), 1)
    return jnp.where(lane < HEAD_DIM, o[:tq], o[tq:])


def _split_pair_t(q):
    return _split_pair(q).T


def _merge_pair_t(o):
    tq = o.shape[1] // 2
    return jnp.concatenate([o[:HEAD_DIM, :tq], o[HEAD_DIM:, tq:]], axis=0).T


class _Softmax:
    def __init__(self, n_queries):
        self.m = jnp.full((1, n_queries), -jnp.inf, jnp.float32)
        self.l = jnp.zeros((1, n_queries), jnp.float32)

    def step(self, s):
        m_new = jnp.maximum(self.m, jnp.max(s, axis=0, keepdims=True))
        alpha = jnp.exp(self.m - m_new)
        p = jnp.exp(s - m_new)
        self.l = alpha * self.l + jnp.sum(p, axis=0, keepdims=True)
        self.m = m_new
        return alpha, p.astype(jnp.bfloat16)


def _chunks(length):
    return [(start, min(KV_CHUNK, length - start)) for start in range(0, length, KV_CHUNK)]


def _pair_attn_kernel(*refs, n_src):
    q_ref = refs[0]
    kv_refs = refs[1:1 + 2 * n_src]
    o_ref = refs[1 + 2 * n_src]
    qt = _split_pair_t(q_ref[0])
    sm = _Softmax(qt.shape[1])
    acc = jnp.zeros((LANES, qt.shape[1]), jnp.float32)
    for s in range(n_src):
        k_ref, vt_ref = kv_refs[2 * s], kv_refs[2 * s + 1]
        for start, size in _chunks(k_ref.shape[1]):
            scores = jnp.dot(k_ref[0, start:start + size, :], qt, preferred_element_type=jnp.float32)
            alpha, p = sm.step(scores)
            acc = alpha * acc + jnp.dot(vt_ref[0, :, start:start + size], p, preferred_element_type=jnp.float32)
    o_ref[0] = _merge_pair_t(acc * (1.0 / sm.l)).astype(o_ref.dtype)


def _pair_attention(q_arr, q_block0, sources, n_pairs, k_of_pair, *, tq):
    bsz, t, _ = q_arr.shape
    in_specs = [pl.BlockSpec((1, tq, LANES), lambda b, p, i: (b, i, q_block0 + p))]
    args = [q_arr]
    for k_arr, kb, vt_arr in sources:
        tk = k_arr.shape[1]
        in_specs.append(pl.BlockSpec((1, tk, LANES), lambda b, p, i, kb=kb: (b, 0, kb + k_of_pair(p))))
        in_specs.append(pl.BlockSpec((1, LANES, tk), lambda b, p, i: (b, k_of_pair(p), 0)))
        args += [k_arr, vt_arr]
    return pl.pallas_call(
        functools.partial(_pair_attn_kernel, n_src=len(sources)),
        grid=(bsz, n_pairs, t // tq),
        in_specs=in_specs,
        out_specs=pl.BlockSpec((1, tq, LANES), lambda b, p, i: (b, i, p)),
        out_shape=jax.ShapeDtypeStruct((bsz, t, n_pairs * LANES), jnp.bfloat16),
        compiler_params=_params("parallel", "parallel", "arbitrary"),
        name="pair_attention",
    )(*args)


def _diff_attn_kernel(*refs, n_src, lam_init):
    q1_ref, q2_ref, lam_ref, g_ref = refs[:4]
    kv_refs = refs[4:4 + 4 * n_src]
    o_ref = refs[4 + 4 * n_src]
    tq = q1_ref.shape[1]
    qts = (_split_pair_t(q1_ref[0]), _split_pair_t(q2_ref[0]))
    sms = (_Softmax(2 * tq), _Softmax(2 * tq))
    accs = [jnp.zeros((LANES, 2 * tq), jnp.float32) for _ in range(2)]
    for s in range(n_src):
        k_refs = kv_refs[4 * s:4 * s + 2]
        vlo_ref, vhi_ref = kv_refs[4 * s + 2:4 * s + 4]
        for start, size in _chunks(k_refs[0].shape[1]):
            v_lo = vlo_ref[0, :, start:start + size]
            v_hi = vhi_ref[0, :, start:start + size]
            for i in range(2):
                scores = jnp.dot(k_refs[i][0, start:start + size, :], qts[i], preferred_element_type=jnp.float32)
                alpha, p = sms[i].step(scores)
                pv = jnp.concatenate([
                    jnp.dot(v_lo, p[:, :tq], preferred_element_type=jnp.float32),
                    jnp.dot(v_hi, p[:, tq:], preferred_element_type=jnp.float32)], axis=1)
                accs[i] = alpha * accs[i] + pv
    lam = lam_ref[...]
    lam_full = (jnp.exp(jnp.sum(lam[0:1] * lam[1:2], axis=-1, keepdims=True))
                - jnp.exp(jnp.sum(lam[2:3] * lam[3:4], axis=-1, keepdims=True)) + lam_init)
    o = accs[0] * (1.0 / sms[0].l) - lam_full * (accs[1] * (1.0 / sms[1].l))
    o = o * lax.rsqrt(jnp.mean(o * o, axis=0, keepdims=True) + EPS) * g_ref[...] * (1.0 - lam_init)
    o_ref[0, :, :LANES] = o[:, :tq].T.astype(o_ref.dtype)
    o_ref[0, :, LANES:] = o[:, tq:].T.astype(o_ref.dtype)


def _diff_attention(q_arr, sources, lam, subln_g, lam_init, *, tq):
    bsz, t, _ = q_arr.shape
    n_pairs = 4
    in_specs = [
        pl.BlockSpec((1, tq, LANES), lambda b, p, i: (b, i, p)),
        pl.BlockSpec((1, tq, LANES), lambda b, p, i: (b, i, n_pairs + p)),
        pl.BlockSpec(lam.shape, lambda b, p, i: (0, 0)),
        pl.BlockSpec(subln_g.shape, lambda b, p, i: (0, 0)),
    ]
    args = [q_arr, q_arr, lam, subln_g]
    for k_arr, vt_arr in sources:
        tk = k_arr.shape[1]
        in_specs += [
            pl.BlockSpec((1, tk, LANES), lambda b, p, i: (b, 0, 2 * n_pairs + p)),
            pl.BlockSpec((1, tk, LANES), lambda b, p, i: (b, 0, 3 * n_pairs + p)),
            pl.BlockSpec((1, LANES, tk), lambda b, p, i: (b, 2 * p, 0)),
            pl.BlockSpec((1, LANES, tk), lambda b, p, i: (b, 2 * p + 1, 0)),
        ]
        args += [k_arr, k_arr, vt_arr, vt_arr]
    return pl.pallas_call(
        functools.partial(_diff_attn_kernel, n_src=len(sources), lam_init=lam_init),
        grid=(bsz, n_pairs, t // tq),
        in_specs=in_specs,
        out_specs=pl.BlockSpec((1, tq, 2 * LANES), lambda b, p, i: (b, i, p)),
        out_shape=jax.ShapeDtypeStruct((bsz, t, 2 * LANES * n_pairs), jnp.bfloat16),
        compiler_params=_params("parallel", "parallel", "arbitrary"),
        name="diff_attention",
    )(*args)


def _na_attn_kernel(q_ref, k_ref, v_ref, kc_ref, vct_ref, bias_ref, o_ref, *, n_rows):
    step = pl.program_id(2)
    tg = NA_GROUP_ROWS * GRID_W
    n_groups = NA_ROWS_PER_STEP // NA_GROUP_ROWS
    win_rows = NA_WIN_H + NA_GROUP_ROWS - 1
    half = NA_WIN_H // 2
    qts = [_split_pair_t(q_ref[0, g * tg:(g + 1) * tg, :]) for g in range(n_groups)]
    s_cx_all = jnp.dot(kc_ref[0], jnp.concatenate(qts, axis=1), preferred_element_type=jnp.float32)
    lane = lax.broadcasted_iota(jnp.int32, (1, 2 * tg), 1)
    first_row = (lane & GRID_W) == 0
    for g in range(n_groups):
        r0 = step * NA_ROWS_PER_STEP + NA_GROUP_ROWS * g
        rs0 = jnp.clip(r0 - half, 0, n_rows - NA_WIN_H)
        rs1 = jnp.clip(r0 + 1 - half, 0, n_rows - NA_WIN_H)
        ws = jnp.minimum(rs0, n_rows - win_rows)
        tok0 = pl.multiple_of(ws * GRID_W, GRID_W)
        k_nb = k_ref[0, pl.ds(tok0, win_rows * GRID_W), :]
        vt_nb = v_ref[0, pl.ds(tok0, win_rows * GRID_W), :].T
        pieces = []
        for kr in range(win_rows):
            key_row = ws + kr
            e = jnp.clip(key_row - r0 + NA_WIN_H - 1, 0, 2 * NA_WIN_H - 1)
            out0 = jnp.where((key_row >= rs0) & (key_row < rs0 + NA_WIN_H), 0.0, MASK_VALUE)
            out1 = jnp.where((key_row >= rs1) & (key_row < rs1 + NA_WIN_H), 0.0, MASK_VALUE)
            pieces.append(bias_ref[0, e] + jnp.where(first_row, out0, out1))
        s_nb = jnp.dot(k_nb, qts[g], preferred_element_type=jnp.float32) + jnp.concatenate(pieces, axis=0)
        s_cx = s_cx_all[:, g * 2 * tg:(g + 1) * 2 * tg]
        m = jnp.maximum(jnp.max(s_nb, axis=0, keepdims=True), jnp.max(s_cx, axis=0, keepdims=True))
        p_nb = jnp.exp(s_nb - m)
        p_cx = jnp.exp(s_cx - m)
        l = jnp.sum(p_nb, axis=0, keepdims=True) + jnp.sum(p_cx, axis=0, keepdims=True)
        o = (jnp.dot(vt_nb, p_nb.astype(jnp.bfloat16), preferred_element_type=jnp.float32)
             + jnp.dot(vct_ref[0], p_cx.astype(jnp.bfloat16), preferred_element_type=jnp.float32))
        o_ref[0, g * tg:(g + 1) * tg, :] = _merge_pair_t(o * (1.0 / l)).astype(o_ref.dtype)


def _na_attention(qkv, qk_ctx, vt_ctx, bias):
    bsz, s, _ = qkv.shape
    l_ctx = qk_ctx.shape[1]
    n_pairs = bias.shape[0]
    tq = NA_ROWS_PER_STEP * GRID_W
    return pl.pallas_call(
        functools.partial(_na_attn_kernel, n_rows=s // GRID_W),
        grid=(bsz, n_pairs, s // tq),
        in_specs=[
            pl.BlockSpec((1, tq, LANES), lambda b, p, i: (b, i, p)),
            pl.BlockSpec((1, s, LANES), lambda b, p, i: (b, 0, n_pairs + p)),
            pl.BlockSpec((1, s, LANES), lambda b, p, i: (b, 0, 2 * n_pairs + p)),
            pl.BlockSpec((1, l_ctx, LANES), lambda b, p, i: (b, 0, n_pairs + p)),
            pl.BlockSpec((1, LANES, l_ctx), lambda b, p, i: (b, p, 0)),
            pl.BlockSpec((1,) + bias.shape[1:], lambda b, p, i: (p, 0, 0, 0)),
        ],
        out_specs=pl.BlockSpec((1, tq, LANES), lambda b, p, i: (b, i, p)),
        out_shape=jax.ShapeDtypeStruct((bsz, s, n_pairs * LANES), jnp.bfloat16),
        compiler_params=_params("parallel", "parallel", "arbitrary"),
        name="na_attention",
    )(qkv, qkv, qkv, qk_ctx, vt_ctx, bias)


def _rope_tables(n):
    t = jnp.arange(n, dtype=jnp.int32)
    row = (t // GRID_W).astype(jnp.float32)
    col = (t % GRID_W).astype(jnp.float32)
    nf = HEAD_DIM // 4
    inv = 1.0 / (ROPE_THETA ** (jnp.arange(nf, dtype=jnp.float32) / nf))
    ar = row[:, None] * inv
    ac = col[:, None] * inv
    zero = jnp.zeros_like(ar)
    cos = jnp.concatenate([jnp.cos(ar), jnp.cos(ar), jnp.cos(ac), jnp.cos(ac)], axis=-1)
    sin_up = jnp.concatenate([-jnp.sin(ar), zero, -jnp.sin(ac), zero], axis=-1)
    sin_down = jnp.concatenate([zero, jnp.sin(ar), zero, jnp.sin(ac)], axis=-1)
    return tuple(jnp.tile(x, (1, LANES // HEAD_DIM)) for x in (cos, sin_up, sin_down))


def _na_bias_table(rpb):
    n_rel = 2 * NA_WIN_H - 1
    cpos = jnp.arange(GRID_W, dtype=jnp.int32)
    cidx = jnp.clip(cpos[None, :] - cpos[:, None] + NA_WIN_W - 1, 0, 2 * NA_WIN_W - 2)
    cstart = jnp.clip(cpos - NA_WIN_W // 2, 0, GRID_W - NA_WIN_W)
    col_ok = (cpos[None, :] >= cstart[:, None]) & (cpos[None, :] < cstart[:, None] + NA_WIN_W)
    t = jnp.where(col_ok, rpb[:, :, cidx], MASK_VALUE)
    t = jnp.swapaxes(t, 2, 3)
    e = jnp.arange(n_rel + 1)
    both = jnp.concatenate([t[:, jnp.clip(e, 0, n_rel - 1)], t[:, jnp.clip(e - 1, 0, n_rel - 1)]], axis=-1)
    n_heads = both.shape[0]
    both = both.reshape(n_heads // 2, 2, n_rel + 1, GRID_W, 2 * GRID_W).transpose(0, 2, 3, 1, 4)
    return both.reshape(n_heads // 2, n_rel + 1, GRID_W, 4 * GRID_W)


def _dup_heads(w):
    d, n = w.shape
    return jnp.tile(w.reshape(d, n // HEAD_DIM, 1, HEAD_DIM), (1, 1, 2, 1)).reshape(d, 2 * n)


def _tile_gain(g):
    return jnp.tile(g, LANES // HEAD_DIM)


def kernel(x, c, ctx, c_ctx, w_mod, b_mod, norm_g, w_ffn_in, w_ffn_out, na_w_qkv, na_rpb, na_w_o, gqa_w_qkv, gqa_q_gain, gqa_k_gain, gqa_w_o, diff_w_qkv, diff_lam, diff_subln_g, diff_w_o):
    bsz, seq, d = x.shape
    l_ctx = ctx.shape[1]
    depth = w_mod.shape[0]
    bf16 = jnp.bfloat16
    tm = 512
    tm_ctx = l_ctx
    tq = 256
    n_pairs = d // LANES

    n_cond = 8 * ((bsz + 1 + 7) // 8)
    cond = jnp.concatenate([c, c_ctx[None], jnp.zeros((n_cond - bsz - 1, d), c.dtype)], axis=0)
    mod_all = _modulation(cond, w_mod, b_mod).reshape(depth, n_cond, N_MOD, d)
    ctx_row = bsz

    rope = _rope_tables(seq)
    no_rope = tuple(jnp.zeros((l_ctx, LANES), jnp.float32) for _ in range(3))
    no_gain = jnp.ones((2, LANES), jnp.float32)
    value_t = _Ep(token_major=False, feature_major=True)

    h, hc = x, ctx
    for i in range(depth):
        kind, j, last = i % 3, i // 3, i == depth - 1
        mod, g = mod_all[i], norm_g[i]
        win = w_ffn_in[i].astype(bf16)
        wout = w_ffn_out[i].astype(bf16)

        h = _ffn(h, mod, None, g, win[0], wout[0], which=0, tm=tm)
        hc = _ffn(hc, mod, ctx_row, g, win[0], wout[0], which=0, tm=tm_ctx)

        if kind == 0:
            w = na_w_qkv[j].astype(bf16)
            ep = [_Ep(qscale=True)] * n_pairs + [_Ep()] * (2 * n_pairs)
            ep_c = [_Ep(qscale=True)] * n_pairs + [_Ep()] * n_pairs + [value_t] * n_pairs
            qkv, _ = _project(h, mod, None, g, w, no_gain, rope, ep, tm=tm)
            qkv_c, vt_c = _project(hc, mod, ctx_row, g, w, no_gain, no_rope, ep_c, tm=tm_ctx)
            o = _na_attention(qkv, qkv_c, vt_c, _na_bias_table(na_rpb[j]))
            if not last:
                oc = _pair_attention(qkv_c, 0, [(qkv_c, n_pairs, vt_c)], n_pairs, lambda p: p, tq=l_ctx)
            w_o = na_w_o[j].astype(bf16)
        elif kind == 1:
            wf = gqa_w_qkv[j]
            dkv = (wf.shape[1] - d) // 2
            w = jnp.concatenate([wf[:, :d], _dup_heads(wf[:, d:d + dkv]), _dup_heads(wf[:, d + dkv:])],
                                axis=1).astype(bf16)
            gains = jnp.stack([_tile_gain(gqa_q_gain[j]), _tile_gain(gqa_k_gain[j])])
            n_kv = dkv // HEAD_DIM
            rep_pairs = n_pairs // n_kv
            ep = [_Ep(norm=0, rope=True, qscale=True)] * n_pairs + [_Ep(norm=1, rope=True)] * n_kv + [value_t] * n_kv
            ep_c = [_Ep(norm=0, qscale=True)] * n_pairs + [_Ep(norm=1)] * n_kv + [value_t] * n_kv
            qk, vt = _project(h, mod, None, g, w, gains, rope, ep, tm=tm)
            qk_c, vt_c = _project(hc, mod, ctx_row, g, w, gains, no_rope, ep_c, tm=tm_ctx)
            o = _pair_attention(qk, 0, [(qk, n_pairs, vt), (qk_c, n_pairs, vt_c)], n_pairs,
                                lambda p: p // rep_pairs, tq=tq)
            if not last:
                oc = _pair_attention(qk_c, 0, [(qk_c, n_pairs, vt_c)], n_pairs, lambda p: p // rep_pairs, tq=l_ctx)
            w_o = gqa_w_o[j].astype(bf16)
        else:
            w = diff_w_qkv[j].astype(bf16)
            lam_init = 0.8 - 0.6 * math.exp(-0.3 * i)
            ep = [_Ep(rope=True, qscale=True)] * n_pairs + [_Ep(rope=True)] * n_pairs + [value_t] * n_pairs
            ep_c = [_Ep(qscale=True)] * n_pairs + [_Ep()] * n_pairs + [value_t] * n_pairs
            qk, vt = _project(h, mod, None, g, w, no_gain, rope, ep, tm=tm)
            qk_c, vt_c = _project(hc, mod, ctx_row, g, w, no_gain, no_rope, ep_c, tm=tm_ctx)
            sub_g = diff_subln_g[j][:, None]
            o = _diff_attention(qk, [(qk, vt), (qk_c, vt_c)], diff_lam[j], sub_g, lam_init, tq=tq)
            if not last:
                oc = _diff_attention(qk_c, [(qk_c, vt_c)], diff_lam[j], sub_g, lam_init, tq=l_ctx)
            w_o = diff_w_o[j].astype(bf16)

        h = _oproj(h, o, mod, None, g, w_o, tm=tm)
        h = _ffn(h, mod, None, g, win[1], wout[1], which=1, tm=tm)
        if not last:
            hc = _oproj(hc, oc, mod, ctx_row, g, w_o, tm=tm_ctx)
            hc = _ffn(hc, mod, ctx_row, g, win[1], wout[1], which=1, tm=tm_ctx)
    return h
```

```python
import functools
import math
from typing import NamedTuple, Optional

import jax
import jax.numpy as jnp
from jax import lax
from jax.experimental import pallas as pl
from jax.experimental.pallas import tpu as pltpu

HEAD_DIM = 64
LANES = 128
GRID_W = 64
NA_WIN_H = 8
NA_WIN_W = 16
ROPE_THETA = 10000.0
EPS = 1e-6
N_MOD = 9
MASK_VALUE = -1e30
LOG2_E = math.log2(math.e)
QUERY_SCALE = HEAD_DIM ** -0.5 * LOG2_E
SUM_ROWS = 16
VT_ROWS = LANES + SUM_ROWS
VMEM_LIMIT_BYTES = 56 * 1024 * 1024

FFN_CHUNK = 256
PROJ_CHUNK = 256
KV_CHUNK = 256
SCORE_LOOKAHEAD = 5
NA_ROWS_PER_STEP = 8
NA_GROUP_ROWS = 2

_NT = (((1,), (1,)), ((), ()))


def _params(*semantics):
    return pltpu.CompilerParams(dimension_semantics=semantics, vmem_limit_bytes=VMEM_LIMIT_BYTES)


def _resident(block_shape, index_map):
    return pl.BlockSpec(block_shape, index_map, pipeline_mode=pl.Buffered(1))


def _rms(x, g):
    return x * lax.rsqrt(jnp.mean(x * x, axis=-1, keepdims=True) + EPS) * g


def _pre(x, g, shift, scale):
    return _rms(x, g) * (1.0 + scale) + shift


def _mod_kernel(c_ref, w_ref, b_ref, o_ref):
    c = c_ref[...]
    sc = (c * jax.nn.sigmoid(c)).astype(jnp.bfloat16)
    w = w_ref[0].astype(jnp.bfloat16)
    o_ref[0] = jnp.dot(sc, w, preferred_element_type=jnp.float32) + b_ref[0]


def _modulation(cond, w_mod, b_mod):
    depth, d, n = w_mod.shape
    r = cond.shape[0]
    tn = n // N_MOD
    return pl.pallas_call(
        _mod_kernel,
        grid=(depth, n // tn),
        in_specs=[
            pl.BlockSpec((r, d), lambda i, j: (0, 0)),
            pl.BlockSpec((1, d, tn), lambda i, j: (i, 0, j)),
            pl.BlockSpec((1, 1, tn), lambda i, j: (i, 0, j)),
        ],
        out_specs=pl.BlockSpec((1, r, tn), lambda i, j: (i, 0, j)),
        out_shape=jax.ShapeDtypeStruct((depth, r, n), jnp.float32),
        compiler_params=_params("arbitrary", "arbitrary"),
        name="modulation",
    )(cond, w_mod, b_mod.reshape(depth, 1, n))


def _ffn_kernel(x_ref, mod_ref, g_ref, win_ref, wout_ref, o_ref, act_ref, *, mod_row, g_row):
    d_ff = wout_ref.shape[0]
    x = x_ref[0]
    shift = mod_ref[0, mod_row:mod_row + 1, :]
    scale = mod_ref[0, mod_row + 1:mod_row + 2, :]
    gate = mod_ref[0, mod_row + 2:mod_row + 3, :]
    xm = _pre(x, g_ref[g_row:g_row + 1, :], shift, scale).astype(jnp.bfloat16)
    for j in range(d_ff // FFN_CHUNK):
        lo = j * FFN_CHUNK
        a = jnp.dot(xm, win_ref[:, lo:lo + FFN_CHUNK], preferred_element_type=jnp.float32)
        b = jnp.dot(xm, win_ref[:, d_ff + lo:d_ff + lo + FFN_CHUNK], preferred_element_type=jnp.float32)
        act_ref[:, lo:lo + FFN_CHUNK] = (a * jax.nn.sigmoid(a) * b).astype(jnp.bfloat16)
    y = jnp.dot(act_ref[...], wout_ref[...], preferred_element_type=jnp.float32)
    o_ref[0] = x + 0.5 * gate * _rms(y, g_ref[g_row + 1:g_row + 2, :])


def _mod_spec(d, cond_row):
    if cond_row is None:
        return pl.BlockSpec((1, N_MOD, d), lambda b, i: (b, 0, 0))
    return pl.BlockSpec((1, N_MOD, d), lambda b, i: (cond_row, 0, 0))


def _ffn(h, mod, cond_row, g, win, wout, *, which, tm):
    bsz, t, d = h.shape
    d_ff = wout.shape[0]
    kern = functools.partial(_ffn_kernel, mod_row=6 * which, g_row=4 * which)
    return pl.pallas_call(
        kern,
        grid=(bsz, t // tm),
        in_specs=[
            pl.BlockSpec((1, tm, d), lambda b, i: (b, i, 0)),
            _mod_spec(d, cond_row),
            _resident(g.shape, lambda b, i: (0, 0)),
            _resident(win.shape, lambda b, i: (0, 0)),
            _resident(wout.shape, lambda b, i: (0, 0)),
        ],
        out_specs=pl.BlockSpec((1, tm, d), lambda b, i: (b, i, 0)),
        out_shape=jax.ShapeDtypeStruct(h.shape, h.dtype),
        scratch_shapes=[pltpu.VMEM((tm, d_ff), jnp.bfloat16)],
        compiler_params=_params("parallel", "parallel"),
        name=f"ffn{which}",
    )(h, mod, g, win, wout)


class _Ep(NamedTuple):
    norm: Optional[int] = None
    rope: bool = False
    qscale: bool = False
    token_major: bool = True
    feature_major: bool = False


def _head_rms(y, gain):
    lane = lax.broadcasted_iota(jnp.int32, y.shape, 1)
    low = lane < HEAD_DIM
    y2 = y * y
    ss_low = jnp.sum(jnp.where(low, y2, 0.0), axis=-1, keepdims=True)
    ss_high = jnp.sum(jnp.where(low, 0.0, y2), axis=-1, keepdims=True)
    r = jnp.where(low, lax.rsqrt(ss_low / HEAD_DIM + EPS), lax.rsqrt(ss_high / HEAD_DIM + EPS))
    return y * r * gain


def _rotary(y, cos, sin_up, sin_down):
    up = pltpu.roll(y, LANES - HEAD_DIM // 4, axis=1)
    down = pltpu.roll(y, HEAD_DIM // 4, axis=1)
    return y * cos + up * sin_up + down * sin_down


def _proj_kernel(x_ref, mod_ref, g_ref, w_ref, gain_ref, cos_ref, sup_ref, sdn_ref, *out_refs, epilogues):
    o_ref = out_refs[0]
    x = x_ref[0]
    shift = mod_ref[0, 3:4, :]
    scale = mod_ref[0, 4:5, :]
    xm = _pre(x, g_ref[2:3, :], shift, scale).astype(jnp.bfloat16)
    n_blocks = len(epilogues)
    per_dot = PROJ_CHUNK // LANES
    main_at = 0
    feat_at = 0
    for c in range(0, n_blocks, per_dot):
        width = min(per_dot, n_blocks - c) * LANES
        y_all = jnp.dot(xm, w_ref[:, c * LANES:c * LANES + width], preferred_element_type=jnp.float32)
        for s in range(width // LANES):
            ep = epilogues[c + s]
            y = y_all[:, s * LANES:(s + 1) * LANES]
            if ep.norm is not None:
                y = _head_rms(y, gain_ref[ep.norm:ep.norm + 1, :])
            if ep.rope:
                y = _rotary(y, cos_ref[...], sup_ref[...], sdn_ref[...])
            if ep.qscale:
                y = y * QUERY_SCALE
            if ep.token_major:
                o_ref[0, :, main_at * LANES:(main_at + 1) * LANES] = y.astype(o_ref.dtype)
                main_at += 1
            if ep.feature_major:
                vt_ref = out_refs[1]
                vt_ref[0, feat_at * VT_ROWS:feat_at * VT_ROWS + LANES, :] = y.T.astype(vt_ref.dtype)
                vt_ref[0, feat_at * VT_ROWS + LANES:(feat_at + 1) * VT_ROWS, :] = jnp.ones(
                    (SUM_ROWS, y.shape[0]), vt_ref.dtype)
                feat_at += 1


def _project(h, mod, cond_row, g, w, gains, rope, epilogues, *, tm):
    bsz, t, d = h.shape
    n_main = sum(ep.token_major for ep in epilogues)
    n_feat = sum(ep.feature_major for ep in epilogues)
    cos, sin_up, sin_down = rope
    rope_spec = pl.BlockSpec((tm, LANES), lambda b, i: (i, 0))
    out_specs = [pl.BlockSpec((1, tm, n_main * LANES), lambda b, i: (b, i, 0))]
    out_shape = [jax.ShapeDtypeStruct((bsz, t, n_main * LANES), jnp.bfloat16)]
    if n_feat:
        out_specs.append(pl.BlockSpec((1, n_feat * VT_ROWS, tm), lambda b, i: (b, 0, i)))
        out_shape.append(jax.ShapeDtypeStruct((bsz, n_feat * VT_ROWS, t), jnp.bfloat16))
    outs = pl.pallas_call(
        functools.partial(_proj_kernel, epilogues=tuple(epilogues)),
        grid=(bsz, t // tm),
        in_specs=[
            pl.BlockSpec((1, tm, d), lambda b, i: (b, i, 0)),
            _mod_spec(d, cond_row),
            _resident(g.shape, lambda b, i: (0, 0)),
            _resident(w.shape, lambda b, i: (0, 0)),
            _resident(gains.shape, lambda b, i: (0, 0)),
            rope_spec, rope_spec, rope_spec,
        ],
        out_specs=out_specs,
        out_shape=out_shape,
        compiler_params=_params("parallel", "parallel"),
        name="qkv_proj",
    )(h, mod, g, w, gains, cos, sin_up, sin_down)
    return outs if n_feat else (outs[0], None)


def _oproj_kernel(h_ref, o_ref, mod_ref, g_ref, w_ref, out_ref):
    y = jnp.dot(o_ref[0], w_ref[...], preferred_element_type=jnp.float32)
    out_ref[0] = h_ref[0] + mod_ref[0, 5:6, :] * _rms(y, g_ref[3:4, :])


def _oproj(h, o, mod, cond_row, g, w, *, tm):
    bsz, t, d = h.shape
    return pl.pallas_call(
        _oproj_kernel,
        grid=(bsz, t // tm),
        in_specs=[
            pl.BlockSpec((1, tm, d), lambda b, i: (b, i, 0)),
            pl.BlockSpec((1, tm, d), lambda b, i: (b, i, 0)),
            _mod_spec(d, cond_row),
            _resident(g.shape, lambda b, i: (0, 0)),
            _resident(w.shape, lambda b, i: (0, 0)),
        ],
        out_specs=pl.BlockSpec((1, tm, d), lambda b, i: (b, i, 0)),
        out_shape=jax.ShapeDtypeStruct(h.shape, h.dtype),
        compiler_params=_params("parallel", "parallel"),
        name="out_proj",
    )(h, o, mod, g, w)


def _split_pair(q):
    lane = lax.broadcasted_iota(jnp.int32, q.shape, 1)
    zero = jnp.zeros_like(q)
    return jnp.concatenate([jnp.where(lane < HEAD_DIM, q, zero), jnp.where(lane < HEAD_DIM, zero, q)], axis=0)


def _merge_pair(o):
    tq = o.shape[0] // 2
    lane = lax.broadcasted_iota(jnp.int32, (tq, LANES), 1)
    return jnp.where(lane < HEAD_DIM, o[:tq], o[tq:])


def _split_pair_t(q):
    return _split_pair(q).T


def _merge_pair_t(o):
    tq = o.shape[1] // 2
    return jnp.concatenate([o[:HEAD_DIM, :tq], o[HEAD_DIM:, tq:]], axis=0).T


class _OnlineSoftmax:
    def __init__(self, n_queries):
        self.m = jnp.full((1, n_queries), -jnp.inf, jnp.float32)
        self.acc = jnp.zeros((VT_ROWS, n_queries), jnp.float32)

    def probs(self, s):
        m_new = jnp.maximum(self.m, jnp.max(s, axis=0, keepdims=True))
        self.acc = self.acc * jnp.exp2(self.m - m_new)
        self.m = m_new
        return jnp.exp2(s - m_new).astype(jnp.bfloat16)

    def add(self, pv):
        self.acc = self.acc + pv

    def result(self):
        return self.acc[:LANES] * (1.0 / self.acc[LANES:LANES + 1])


def _chunks(length):
    return [(start, min(KV_CHUNK, length - start)) for start in range(0, length, KV_CHUNK)]


def _software_pipeline(n, produce, consume):
    pending = {}
    for c in range(n + SCORE_LOOKAHEAD):
        if c < n:
            pending[c] = produce(c)
        if c >= SCORE_LOOKAHEAD:
            consume(c - SCORE_LOOKAHEAD, pending.pop(c - SCORE_LOOKAHEAD))


def _pair_attn_kernel(*refs, n_src):
    q_ref = refs[0]
    kv_refs = refs[1:1 + 2 * n_src]
    o_ref = refs[1 + 2 * n_src]
    qt = _split_pair_t(q_ref[0])
    sm = _OnlineSoftmax(qt.shape[1])
    work = [(kv_refs[2 * s], kv_refs[2 * s + 1], start, size)
            for s in range(n_src) for start, size in _chunks(kv_refs[2 * s].shape[1])]

    def scores(c):
        k_ref, _, start, size = work[c]
        return jnp.dot(k_ref[0, start:start + size, :], qt, preferred_element_type=jnp.float32)

    def consume(c, s):
        _, vt_ref, start, size = work[c]
        p = sm.probs(s)
        sm.add(jnp.dot(vt_ref[0, :, start:start + size], p, preferred_element_type=jnp.float32))

    _software_pipeline(len(work), scores, consume)
    o_ref[0] = _merge_pair_t(sm.result()).astype(o_ref.dtype)


def _pair_attention(q_arr, q_block0, sources, n_pairs, k_of_pair, *, tq):
    bsz, t, _ = q_arr.shape
    in_specs = [pl.BlockSpec((1, tq, LANES), lambda b, p, i: (b, i, q_block0 + p))]
    args = [q_arr]
    for k_arr, kb, vt_arr in sources:
        tk = k_arr.shape[1]
        in_specs.append(pl.BlockSpec((1, tk, LANES), lambda b, p, i, kb=kb: (b, 0, kb + k_of_pair(p))))
        in_specs.append(pl.BlockSpec((1, VT_ROWS, tk), lambda b, p, i: (b, k_of_pair(p), 0)))
        args += [k_arr, vt_arr]
    return pl.pallas_call(
        functools.partial(_pair_attn_kernel, n_src=len(sources)),
        grid=(bsz, n_pairs, t // tq),
        in_specs=in_specs,
        out_specs=pl.BlockSpec((1, tq, LANES), lambda b, p, i: (b, i, p)),
        out_shape=jax.ShapeDtypeStruct((bsz, t, n_pairs * LANES), jnp.bfloat16),
        compiler_params=_params("parallel", "parallel", "arbitrary"),
        name="pair_attention",
    )(*args)


def _diff_attn_kernel(*refs, n_src, lam_init):
    q1_ref, q2_ref, lam_ref, g_ref = refs[:4]
    kv_refs = refs[4:4 + 4 * n_src]
    o_ref = refs[4 + 4 * n_src]
    tq = q1_ref.shape[1]
    qts = (_split_pair_t(q1_ref[0]), _split_pair_t(q2_ref[0]))
    sms = (_OnlineSoftmax(2 * tq), _OnlineSoftmax(2 * tq))
    work = [(i, kv_refs[4 * s + i], kv_refs[4 * s + 2], kv_refs[4 * s + 3], start, size)
            for s in range(n_src) for start, size in _chunks(kv_refs[4 * s].shape[1]) for i in range(2)]

    def scores(c):
        i, k_ref, _, _, start, size = work[c]
        return jnp.dot(k_ref[0, start:start + size, :], qts[i], preferred_element_type=jnp.float32)

    def consume(c, s):
        i, _, vlo_ref, vhi_ref, start, size = work[c]
        p = sms[i].probs(s)
        sms[i].add(jnp.concatenate([
            jnp.dot(vlo_ref[0, :, start:start + size], p[:, :tq], preferred_element_type=jnp.float32),
            jnp.dot(vhi_ref[0, :, start:start + size], p[:, tq:], preferred_element_type=jnp.float32)], axis=1))

    _software_pipeline(len(work), scores, consume)
    lam = lam_ref[...]
    lam_full = (jnp.exp(jnp.sum(lam[0:1] * lam[1:2], axis=-1, keepdims=True))
                - jnp.exp(jnp.sum(lam[2:3] * lam[3:4], axis=-1, keepdims=True)) + lam_init)
    o = sms[0].result() - lam_full * sms[1].result()
    o = o * lax.rsqrt(jnp.mean(o * o, axis=0, keepdims=True) + EPS) * g_ref[...] * (1.0 - lam_init)
    o_ref[0, :, :LANES] = o[:, :tq].T.astype(o_ref.dtype)
    o_ref[0, :, LANES:] = o[:, tq:].T.astype(o_ref.dtype)


def _diff_attention(q_arr, sources, lam, subln_g, lam_init, *, tq):
    bsz, t, _ = q_arr.shape
    n_pairs = 4
    in_specs = [
        pl.BlockSpec((1, tq, LANES), lambda b, p, i: (b, i, p)),
        pl.BlockSpec((1, tq, LANES), lambda b, p, i: (b, i, n_pairs + p)),
        pl.BlockSpec(lam.shape, lambda b, p, i: (0, 0)),
        pl.BlockSpec(subln_g.shape, lambda b, p, i: (0, 0)),
    ]
    args = [q_arr, q_arr, lam, subln_g]
    for k_arr, vt_arr in sources:
        tk = k_arr.shape[1]
        in_specs += [
            pl.BlockSpec((1, tk, LANES), lambda b, p, i: (b, 0, 2 * n_pairs + p)),
            pl.BlockSpec((1, tk, LANES), lambda b, p, i: (b, 0, 3 * n_pairs + p)),
            pl.BlockSpec((1, VT_ROWS, tk), lambda b, p, i: (b, 2 * p, 0)),
            pl.BlockSpec((1, VT_ROWS, tk), lambda b, p, i: (b, 2 * p + 1, 0)),
        ]
        args += [k_arr, k_arr, vt_arr, vt_arr]
    return pl.pallas_call(
        functools.partial(_diff_attn_kernel, n_src=len(sources), lam_init=lam_init),
        grid=(bsz, n_pairs, t // tq),
        in_specs=in_specs,
        out_specs=pl.BlockSpec((1, tq, 2 * LANES), lambda b, p, i: (b, i, p)),
        out_shape=jax.ShapeDtypeStruct((bsz, t, 2 * LANES * n_pairs), jnp.bfloat16),
        compiler_params=_params("parallel", "parallel", "arbitrary"),
        name="diff_attention",
    )(*args)


def _na_attn_kernel(q_ref, k_ref, v_ref, kc_ref, vct_ref, bias_ref, o_ref, *, n_rows):
    step = pl.program_id(2)
    tg = NA_GROUP_ROWS * GRID_W
    n_groups = NA_ROWS_PER_STEP // NA_GROUP_ROWS
    win_rows = NA_WIN_H + NA_GROUP_ROWS - 1
    half = NA_WIN_H // 2
    qts = [_split_pair_t(q_ref[0, g * tg:(g + 1) * tg, :]) for g in range(n_groups)]
    s_cx_all = jnp.dot(kc_ref[0], jnp.concatenate(qts, axis=1), preferred_element_type=jnp.float32)
    lane = lax.broadcasted_iota(jnp.int32, (1, 2 * tg), 1)
    first_row = (lane & GRID_W) == 0

    def window(g):
        r0 = step * NA_ROWS_PER_STEP + NA_GROUP_ROWS * g
        rs0 = jnp.clip(r0 - half, 0, n_rows - NA_WIN_H)
        rs1 = jnp.clip(r0 + 1 - half, 0, n_rows - NA_WIN_H)
        ws = jnp.minimum(rs0, n_rows - win_rows)
        return r0, rs0, rs1, ws, pl.multiple_of(ws * GRID_W, GRID_W)

    def scores(g):
        r0, rs0, rs1, ws, tok0 = window(g)
        pieces = []
        for kr in range(win_rows):
            key_row = ws + kr
            e = jnp.clip(key_row - r0 + NA_WIN_H - 1, 0, 2 * NA_WIN_H - 1)
            out0 = jnp.where((key_row >= rs0) & (key_row < rs0 + NA_WIN_H), 0.0, MASK_VALUE)
            out1 = jnp.where((key_row >= rs1) & (key_row < rs1 + NA_WIN_H), 0.0, MASK_VALUE)
            pieces.append(bias_ref[0, e] + jnp.where(first_row, out0, out1))
        k_nb = k_ref[0, pl.ds(tok0, win_rows * GRID_W), :]
        return jnp.dot(k_nb, qts[g], preferred_element_type=jnp.float32) + jnp.concatenate(pieces, axis=0)

    def consume(g, s_nb):
        tok0 = window(g)[-1]
        vt_nb = jnp.concatenate([v_ref[0, pl.ds(tok0, win_rows * GRID_W), :].T,
                                 jnp.ones((SUM_ROWS, win_rows * GRID_W), jnp.bfloat16)], axis=0)
        s_cx = s_cx_all[:, g * 2 * tg:(g + 1) * 2 * tg]
        m = jnp.maximum(jnp.max(s_nb, axis=0, keepdims=True), jnp.max(s_cx, axis=0, keepdims=True))
        p_nb = jnp.exp2(s_nb - m).astype(jnp.bfloat16)
        p_cx = jnp.exp2(s_cx - m).astype(jnp.bfloat16)
        o = (jnp.dot(vt_nb, p_nb, preferred_element_type=jnp.float32)
             + jnp.dot(vct_ref[0], p_cx, preferred_element_type=jnp.float32))
        o = o[:LANES] * (1.0 / o[LANES:LANES + 1])
        o_ref[0, g * tg:(g + 1) * tg, :] = _merge_pair_t(o).astype(o_ref.dtype)

    _software_pipeline(n_groups, scores, consume)


def _na_attention(qkv, qk_ctx, vt_ctx, bias):
    bsz, s, _ = qkv.shape
    l_ctx = qk_ctx.shape[1]
    n_pairs = bias.shape[0]
    tq = NA_ROWS_PER_STEP * GRID_W
    return pl.pallas_call(
        functools.partial(_na_attn_kernel, n_rows=s // GRID_W),
        grid=(bsz, n_pairs, s // tq),
        in_specs=[
            pl.BlockSpec((1, tq, LANES), lambda b, p, i: (b, i, p)),
            pl.BlockSpec((1, s, LANES), lambda b, p, i: (b, 0, n_pairs + p)),
            pl.BlockSpec((1, s, LANES), lambda b, p, i: (b, 0, 2 * n_pairs + p)),
            pl.BlockSpec((1, l_ctx, LANES), lambda b, p, i: (b, 0, n_pairs + p)),
            pl.BlockSpec((1, VT_ROWS, l_ctx), lambda b, p, i: (b, p, 0)),
            pl.BlockSpec((1,) + bias.shape[1:], lambda b, p, i: (p, 0, 0, 0)),
        ],
        out_specs=pl.BlockSpec((1, tq, LANES), lambda b, p, i: (b, i, p)),
        out_shape=jax.ShapeDtypeStruct((bsz, s, n_pairs * LANES), jnp.bfloat16),
        compiler_params=_params("parallel", "parallel", "arbitrary"),
        name="na_attention",
    )(qkv, qkv, qkv, qk_ctx, vt_ctx, bias)


def _rope_tables(n):
    t = jnp.arange(n, dtype=jnp.int32)
    row = (t // GRID_W).astype(jnp.float32)
    col = (t % GRID_W).astype(jnp.float32)
    nf = HEAD_DIM // 4
    inv = 1.0 / (ROPE_THETA ** (jnp.arange(nf, dtype=jnp.float32) / nf))
    ar = row[:, None] * inv
    ac = col[:, None] * inv
    zero = jnp.zeros_like(ar)
    cos = jnp.concatenate([jnp.cos(ar), jnp.cos(ar), jnp.cos(ac), jnp.cos(ac)], axis=-1)
    sin_up = jnp.concatenate([-jnp.sin(ar), zero, -jnp.sin(ac), zero], axis=-1)
    sin_down = jnp.concatenate([zero, jnp.sin(ar), zero, jnp.sin(ac)], axis=-1)
    return tuple(jnp.tile(x, (1, LANES // HEAD_DIM)) for x in (cos, sin_up, sin_down))


def _na_bias_table(rpb):
    n_rel = 2 * NA_WIN_H - 1
    cpos = jnp.arange(GRID_W, dtype=jnp.int32)
    cidx = jnp.clip(cpos[None, :] - cpos[:, None] + NA_WIN_W - 1, 0, 2 * NA_WIN_W - 2)
    cstart = jnp.clip(cpos - NA_WIN_W // 2, 0, GRID_W - NA_WIN_W)
    col_ok = (cpos[None, :] >= cstart[:, None]) & (cpos[None, :] < cstart[:, None] + NA_WIN_W)
    t = jnp.where(col_ok, rpb[:, :, cidx] * LOG2_E, MASK_VALUE)
    t = jnp.swapaxes(t, 2, 3)
    e = jnp.arange(n_rel + 1)
    both = jnp.concatenate([t[:, jnp.clip(e, 0, n_rel - 1)], t[:, jnp.clip(e - 1, 0, n_rel - 1)]], axis=-1)
    n_heads = both.shape[0]
    both = both.reshape(n_heads // 2, 2, n_rel + 1, GRID_W, 2 * GRID_W).transpose(0, 2, 3, 1, 4)
    return both.reshape(n_heads // 2, n_rel + 1, GRID_W, 4 * GRID_W)


def _dup_heads(w):
    d, n = w.shape
    return jnp.tile(w.reshape(d, n // HEAD_DIM, 1, HEAD_DIM), (1, 1, 2, 1)).reshape(d, 2 * n)


def _tile_gain(g):
    return jnp.tile(g, LANES // HEAD_DIM)


def kernel(x, c, ctx, c_ctx, w_mod, b_mod, norm_g, w_ffn_in, w_ffn_out, na_w_qkv, na_rpb, na_w_o, gqa_w_qkv, gqa_q_gain, gqa_k_gain, gqa_w_o, diff_w_qkv, diff_lam, diff_subln_g, diff_w_o):
    bsz, seq, d = x.shape
    l_ctx = ctx.shape[1]
    depth = w_mod.shape[0]
    bf16 = jnp.bfloat16
    tm = 512
    tm_ctx = l_ctx
    tq = 512
    n_pairs = d // LANES

    n_cond = 8 * ((bsz + 1 + 7) // 8)
    cond = jnp.concatenate([c, c_ctx[None], jnp.zeros((n_cond - bsz - 1, d), c.dtype)], axis=0)
    mod_all = _modulation(cond, w_mod, b_mod).reshape(depth, n_cond, N_MOD, d)
    ctx_row = bsz

    rope = _rope_tables(seq)
    no_rope = tuple(jnp.zeros((l_ctx, LANES), jnp.float32) for _ in range(3))
    no_gain = jnp.ones((2, LANES), jnp.float32)
    value_t = _Ep(token_major=False, feature_major=True)

    h, hc = x, ctx
    for i in range(depth):
        kind, j, last = i % 3, i // 3, i == depth - 1
        mod, g = mod_all[i], norm_g[i]
        win = w_ffn_in[i].astype(bf16)
        wout = w_ffn_out[i].astype(bf16)

        h = _ffn(h, mod, None, g, win[0], wout[0], which=0, tm=tm)
        hc = _ffn(hc, mod, ctx_row, g, win[0], wout[0], which=0, tm=tm_ctx)

        if kind == 0:
            w = na_w_qkv[j].astype(bf16)
            ep = [_Ep(qscale=True)] * n_pairs + [_Ep()] * (2 * n_pairs)
            ep_c = [_Ep(qscale=True)] * n_pairs + [_Ep()] * n_pairs + [value_t] * n_pairs
            qkv, _ = _project(h, mod, None, g, w, no_gain, rope, ep, tm=tm)
            qkv_c, vt_c = _project(hc, mod, ctx_row, g, w, no_gain, no_rope, ep_c, tm=tm_ctx)
            o = _na_attention(qkv, qkv_c, vt_c, _na_bias_table(na_rpb[j]))
            if not last:
                oc = _pair_attention(qkv_c, 0, [(qkv_c, n_pairs, vt_c)], n_pairs, lambda p: p, tq=l_ctx)
            w_o = na_w_o[j].astype(bf16)
        elif kind == 1:
            wf = gqa_w_qkv[j]
            dkv = (wf.shape[1] - d) // 2
            w = jnp.concatenate([wf[:, :d], _dup_heads(wf[:, d:d + dkv]), _dup_heads(wf[:, d + dkv:])],
                                axis=1).astype(bf16)
            gains = jnp.stack([_tile_gain(gqa_q_gain[j]), _tile_gain(gqa_k_gain[j])])
            n_kv = dkv // HEAD_DIM
            rep_pairs = n_pairs // n_kv
            ep = [_Ep(norm=0, rope=True, qscale=True)] * n_pairs + [_Ep(norm=1, rope=True)] * n_kv + [value_t] * n_kv
            ep_c = [_Ep(norm=0, qscale=True)] * n_pairs + [_Ep(norm=1)] * n_kv + [value_t] * n_kv
            qk, vt = _project(h, mod, None, g, w, gains, rope, ep, tm=tm)
            qk_c, vt_c = _project(hc, mod, ctx_row, g, w, gains, no_rope, ep_c, tm=tm_ctx)
            o = _pair_attention(qk, 0, [(qk, n_pairs, vt), (qk_c, n_pairs, vt_c)], n_pairs,
                                lambda p: p // rep_pairs, tq=tq)
            if not last:
                oc = _pair_attention(qk_c, 0, [(qk_c, n_pairs, vt_c)], n_pairs, lambda p: p // rep_pairs, tq=l_ctx)
            w_o = gqa_w_o[j].astype(bf16)
        else:
            w = diff_w_qkv[j].astype(bf16)
            lam_init = 0.8 - 0.6 * math.exp(-0.3 * i)
            ep = [_Ep(rope=True, qscale=True)] * n_pairs + [_Ep(rope=True)] * n_pairs + [value_t] * n_pairs
            ep_c = [_Ep(qscale=True)] * n_pairs + [_Ep()] * n_pairs + [value_t] * n_pairs
            qk, vt = _project(h, mod, None, g, w, no_gain, rope, ep, tm=tm)
            qk_c, vt_c = _project(hc, mod, ctx_row, g, w, no_gain, no_rope, ep_c, tm=tm_ctx)
            sub_g = diff_subln_g[j][:, None]
            o = _diff_attention(qk, [(qk, vt), (qk_c, vt_c)], diff_lam[j], sub_g, lam_init, tq=tq)
            if not last:
                oc = _diff_attention(qk_c, [(qk_c, vt_c)], diff_lam[j], sub_g, lam_init, tq=l_ctx)
            w_o = diff_w_o[j].astype(bf16)

        h = _oproj(h, o, mod, None, g, w_o, tm=tm)
        h = _ffn(h, mod, None, g, win[1], wout[1], which=1, tm=tm)
        if not last:
            hc = _oproj(hc, oc, mod, ctx_row, g, w_o, tm=tm_ctx)
            hc = _ffn(hc, mod, ctx_row, g, win[1], wout[1], which=1, tm=tm_ctx)
    return h
```

```python
import functools
import math
from typing import NamedTuple, Optional

import jax
import jax.numpy as jnp
from jax import lax
from jax.experimental import pallas as pl
from jax.experimental.pallas import tpu as pltpu

HEAD_DIM = 64
LANES = 128
GRID_W = 64
NA_WIN_H = 8
NA_WIN_W = 16
ROPE_THETA = 10000.0
EPS = 1e-6
N_MOD = 9
MASK_VALUE = -1e30
LOG2_E = math.log2(math.e)
QUERY_SCALE = HEAD_DIM ** -0.5 * LOG2_E
SUM_ROWS = 16
VT_ROWS = LANES + SUM_ROWS
VMEM_LIMIT_BYTES = 56 * 1024 * 1024

FFN_CHUNK = 256
PROJ_CHUNK = 256
KV_CHUNK = 256
SCORE_LOOKAHEAD = 5
NA_ROWS_PER_STEP = 32
NA_GROUP_ROWS = 2


def _params(*semantics):
    return pltpu.CompilerParams(dimension_semantics=semantics, vmem_limit_bytes=VMEM_LIMIT_BYTES)


def _resident(block_shape, index_map):
    return pl.BlockSpec(block_shape, index_map, pipeline_mode=pl.Buffered(1))


def _rms(x, g):
    return x * lax.rsqrt(jnp.mean(x * x, axis=-1, keepdims=True) + EPS) * g


def _pre(x, g, shift, scale):
    return _rms(x, g) * (1.0 + scale) + shift


def _mod_kernel(c_ref, w_ref, b_ref, o_ref):
    c = c_ref[...]
    sc = (c * jax.nn.sigmoid(c)).astype(jnp.bfloat16)
    w = w_ref[0].astype(jnp.bfloat16)
    o_ref[0] = jnp.dot(sc, w, preferred_element_type=jnp.float32) + b_ref[0]


def _modulation(cond, w_mod, b_mod):
    depth, d, n = w_mod.shape
    r = cond.shape[0]
    tn = n // N_MOD
    return pl.pallas_call(
        _mod_kernel,
        grid=(depth, n // tn),
        in_specs=[
            pl.BlockSpec((r, d), lambda i, j: (0, 0)),
            pl.BlockSpec((1, d, tn), lambda i, j: (i, 0, j)),
            pl.BlockSpec((1, 1, tn), lambda i, j: (i, 0, j)),
        ],
        out_specs=pl.BlockSpec((1, r, tn), lambda i, j: (i, 0, j)),
        out_shape=jax.ShapeDtypeStruct((depth, r, n), jnp.float32),
        compiler_params=_params("arbitrary", "arbitrary"),
        name="modulation",
    )(cond, w_mod, b_mod.reshape(depth, 1, n))


def _ffn_kernel(x_ref, mod_ref, g_ref, win_ref, wout_ref, *rest, mod_row, g_row):
    o_ref, act_ref = rest[-2:]
    d_ff = wout_ref.shape[0]
    x = x_ref[0]
    if len(rest) == 4:
        attn_ref, wo_ref = rest[:2]
        y = jnp.dot(attn_ref[0], wo_ref[...], preferred_element_type=jnp.float32)
        x = x + mod_ref[0, 5:6, :] * _rms(y, g_ref[3:4, :])
    shift = mod_ref[0, mod_row:mod_row + 1, :]
    scale = mod_ref[0, mod_row + 1:mod_row + 2, :]
    gate = mod_ref[0, mod_row + 2:mod_row + 3, :]
    xm = _pre(x, g_ref[g_row:g_row + 1, :], shift, scale).astype(jnp.bfloat16)
    for j in range(d_ff // FFN_CHUNK):
        lo = j * FFN_CHUNK
        a = jnp.dot(xm, win_ref[:, lo:lo + FFN_CHUNK], preferred_element_type=jnp.float32)
        b = jnp.dot(xm, win_ref[:, d_ff + lo:d_ff + lo + FFN_CHUNK], preferred_element_type=jnp.float32)
        act_ref[:, lo:lo + FFN_CHUNK] = (a * jax.nn.sigmoid(a) * b).astype(jnp.bfloat16)
    y = jnp.dot(act_ref[...], wout_ref[...], preferred_element_type=jnp.float32)
    o_ref[0] = x + 0.5 * gate * _rms(y, g_ref[g_row + 1:g_row + 2, :])


def _mod_spec(d, cond_row):
    if cond_row is None:
        return pl.BlockSpec((1, N_MOD, d), lambda b, i: (b, 0, 0))
    return pl.BlockSpec((1, N_MOD, d), lambda b, i: (cond_row, 0, 0))


def _ffn(h, mod, cond_row, g, win, wout, *, which, tm, attn=None, w_o=None):
    bsz, t, d = h.shape
    d_ff = wout.shape[0]
    kern = functools.partial(_ffn_kernel, mod_row=6 * which, g_row=4 * which)
    in_specs = [
        pl.BlockSpec((1, tm, d), lambda b, i: (b, i, 0)),
        _mod_spec(d, cond_row),
        _resident(g.shape, lambda b, i: (0, 0)),
        _resident(win.shape, lambda b, i: (0, 0)),
        _resident(wout.shape, lambda b, i: (0, 0)),
    ]
    args = [h, mod, g, win, wout]
    if attn is not None:
        in_specs += [pl.BlockSpec((1, tm, d), lambda b, i: (b, i, 0)), _resident(w_o.shape, lambda b, i: (0, 0))]
        args += [attn, w_o]
    return pl.pallas_call(
        kern,
        grid=(bsz, t // tm),
        in_specs=in_specs,
        out_specs=pl.BlockSpec((1, tm, d), lambda b, i: (b, i, 0)),
        out_shape=jax.ShapeDtypeStruct(h.shape, h.dtype),
        scratch_shapes=[pltpu.VMEM((tm, d_ff), jnp.bfloat16)],
        compiler_params=_params("parallel", "parallel"),
        name=f"ffn{which}",
    )(*args)


class _Ep(NamedTuple):
    norm: Optional[int] = None
    rope: bool = False
    qscale: bool = False
    token_major: bool = True
    vt_features: int = 0


def _head_rms(y, gain):
    lane = lax.broadcasted_iota(jnp.int32, y.shape, 1)
    low = lane < HEAD_DIM
    y2 = y * y
    ss_low = jnp.sum(jnp.where(low, y2, 0.0), axis=-1, keepdims=True)
    ss_high = jnp.sum(jnp.where(low, 0.0, y2), axis=-1, keepdims=True)
    r = jnp.where(low, lax.rsqrt(ss_low / HEAD_DIM + EPS), lax.rsqrt(ss_high / HEAD_DIM + EPS))
    return y * r * gain


def _rotary(y, cos, sin_up, sin_down):
    up = pltpu.roll(y, LANES - HEAD_DIM // 4, axis=1)
    down = pltpu.roll(y, HEAD_DIM // 4, axis=1)
    return y * cos + up * sin_up + down * sin_down


def _proj_kernel(x_ref, mod_ref, g_ref, w_ref, gain_ref, cos_ref, sup_ref, sdn_ref, *out_refs, epilogues):
    o_ref = out_refs[0]
    x = x_ref[0]
    shift = mod_ref[0, 3:4, :]
    scale = mod_ref[0, 4:5, :]
    xm = _pre(x, g_ref[2:3, :], shift, scale).astype(jnp.bfloat16)
    n_blocks = len(epilogues)
    per_dot = PROJ_CHUNK // LANES
    main_at = 0
    vt_row = 0
    for c in range(0, n_blocks, per_dot):
        width = min(per_dot, n_blocks - c) * LANES
        y_all = jnp.dot(xm, w_ref[:, c * LANES:c * LANES + width], preferred_element_type=jnp.float32)
        for s in range(width // LANES):
            ep = epilogues[c + s]
            y = y_all[:, s * LANES:(s + 1) * LANES]
            if ep.norm is not None:
                y = _head_rms(y, gain_ref[ep.norm:ep.norm + 1, :])
            if ep.rope:
                y = _rotary(y, cos_ref[...], sup_ref[...], sdn_ref[...])
            if ep.qscale:
                y = y * QUERY_SCALE
            if ep.token_major:
                o_ref[0, :, main_at * LANES:(main_at + 1) * LANES] = y.astype(o_ref.dtype)
                main_at += 1
            if ep.vt_features:
                vt_ref = out_refs[1]
                yt = y.T.astype(vt_ref.dtype)
                for f0 in range(0, LANES, ep.vt_features):
                    vt_ref[0, vt_row:vt_row + ep.vt_features, :] = yt[f0:f0 + ep.vt_features]
                    vt_row += ep.vt_features
                    vt_ref[0, vt_row:vt_row + SUM_ROWS, :] = jnp.ones((SUM_ROWS, y.shape[0]), vt_ref.dtype)
                    vt_row += SUM_ROWS


def _project(h, mod, cond_row, g, w, gains, rope, epilogues, *, tm):
    bsz, t, d = h.shape
    n_main = sum(ep.token_major for ep in epilogues)
    vt_rows = sum(LANES // ep.vt_features * (ep.vt_features + SUM_ROWS) for ep in epilogues if ep.vt_features)
    cos, sin_up, sin_down = rope
    rope_spec = pl.BlockSpec((tm, LANES), lambda b, i: (i, 0))
    out_specs = [pl.BlockSpec((1, tm, n_main * LANES), lambda b, i: (b, i, 0))]
    out_shape = [jax.ShapeDtypeStruct((bsz, t, n_main * LANES), jnp.bfloat16)]
    if vt_rows:
        out_specs.append(pl.BlockSpec((1, vt_rows, tm), lambda b, i: (b, 0, i)))
        out_shape.append(jax.ShapeDtypeStruct((bsz, vt_rows, t), jnp.bfloat16))
    outs = pl.pallas_call(
        functools.partial(_proj_kernel, epilogues=tuple(epilogues)),
        grid=(bsz, t // tm),
        in_specs=[
            pl.BlockSpec((1, tm, d), lambda b, i: (b, i, 0)),
            _mod_spec(d, cond_row),
            _resident(g.shape, lambda b, i: (0, 0)),
            _resident(w.shape, lambda b, i: (0, 0)),
            _resident(gains.shape, lambda b, i: (0, 0)),
            rope_spec, rope_spec, rope_spec,
        ],
        out_specs=out_specs,
        out_shape=out_shape,
        compiler_params=_params("parallel", "parallel"),
        name="qkv_proj",
    )(h, mod, g, w, gains, cos, sin_up, sin_down)
    return outs if vt_rows else (outs[0], None)


def _split_pair(q):
    lane = lax.broadcasted_iota(jnp.int32, q.shape, 1)
    zero = jnp.zeros_like(q)
    return jnp.concatenate([jnp.where(lane < HEAD_DIM, q, zero), jnp.where(lane < HEAD_DIM, zero, q)], axis=0)


def _split_pair_t(q):
    return _split_pair(q).T


def _merge_pair_t(o):
    tq = o.shape[1] // 2
    hi = o[HEAD_DIM:, tq:] if o.shape[0] == LANES else o[:, tq:]
    return jnp.concatenate([o[:HEAD_DIM, :tq], hi], axis=0).T


class _OnlineSoftmax:
    def __init__(self, n_queries, features=LANES):
        self.features = features
        self.m = jnp.full((1, n_queries), -jnp.inf, jnp.float32)
        self.acc = jnp.zeros((features + SUM_ROWS, n_queries), jnp.float32)

    def probs(self, s):
        m_new = jnp.maximum(self.m, jnp.max(s, axis=0, keepdims=True))
        self.acc = self.acc * jnp.exp2(self.m - m_new)
        self.m = m_new
        return jnp.exp2(s - m_new).astype(jnp.bfloat16)

    def add(self, pv):
        self.acc = self.acc + pv

    def result(self):
        return self.acc[:self.features] * (1.0 / self.acc[self.features:self.features + 1])


def _chunks(length):
    return [(start, min(KV_CHUNK, length - start)) for start in range(0, length, KV_CHUNK)]


def _software_pipeline(n, produce, consume):
    pending = {}
    for c in range(n + SCORE_LOOKAHEAD):
        if c < n:
            pending[c] = produce(c)
        if c >= SCORE_LOOKAHEAD:
            consume(c - SCORE_LOOKAHEAD, pending.pop(c - SCORE_LOOKAHEAD))


def _pair_attn_kernel(*refs, n_src):
    q_ref = refs[0]
    kv_refs = refs[1:1 + 2 * n_src]
    o_ref = refs[1 + 2 * n_src]
    qt = _split_pair_t(q_ref[0])
    sm = _OnlineSoftmax(qt.shape[1], kv_refs[1].shape[1] - SUM_ROWS)
    work = [(kv_refs[2 * s], kv_refs[2 * s + 1], start, size)
            for s in range(n_src) for start, size in _chunks(kv_refs[2 * s].shape[1])]

    def scores(c):
        k_ref, _, start, size = work[c]
        return jnp.dot(k_ref[0, start:start + size, :], qt, preferred_element_type=jnp.float32)

    def consume(c, s):
        _, vt_ref, start, size = work[c]
        p = sm.probs(s)
        sm.add(jnp.dot(vt_ref[0, :, start:start + size], p, preferred_element_type=jnp.float32))

    _software_pipeline(len(work), scores, consume)
    o_ref[0] = _merge_pair_t(sm.result()).astype(o_ref.dtype)


def _pair_attention(q_arr, q_block0, sources, n_pairs, k_of_pair, *, tq, v_features=LANES):
    bsz, t, _ = q_arr.shape
    in_specs = [pl.BlockSpec((1, tq, LANES), lambda b, p, i: (b, i, q_block0 + p))]
    args = [q_arr]
    for k_arr, kb, vt_arr in sources:
        tk = k_arr.shape[1]
        in_specs.append(pl.BlockSpec((1, tk, LANES), lambda b, p, i, kb=kb: (b, 0, kb + k_of_pair(p))))
        in_specs.append(pl.BlockSpec((1, v_features + SUM_ROWS, tk), lambda b, p, i: (b, k_of_pair(p), 0)))
        args += [k_arr, vt_arr]
    return pl.pallas_call(
        functools.partial(_pair_attn_kernel, n_src=len(sources)),
        grid=(bsz, n_pairs, t // tq),
        in_specs=in_specs,
        out_specs=pl.BlockSpec((1, tq, LANES), lambda b, p, i: (b, i, p)),
        out_shape=jax.ShapeDtypeStruct((bsz, t, n_pairs * LANES), jnp.bfloat16),
        compiler_params=_params("parallel", "parallel", "arbitrary"),
        name="pair_attention",
    )(*args)


def _diff_attn_kernel(*refs, n_src, lam_init):
    q1_ref, q2_ref, lam_ref, g_ref = refs[:4]
    kv_refs = refs[4:4 + 4 * n_src]
    o_ref = refs[4 + 4 * n_src]
    tq = q1_ref.shape[1]
    qts = (_split_pair_t(q1_ref[0]), _split_pair_t(q2_ref[0]))
    sms = (_OnlineSoftmax(2 * tq), _OnlineSoftmax(2 * tq))
    work = [(i, kv_refs[4 * s + i], kv_refs[4 * s + 2], kv_refs[4 * s + 3], start, size)
            for s in range(n_src) for start, size in _chunks(kv_refs[4 * s].shape[1]) for i in range(2)]

    def scores(c):
        i, k_ref, _, _, start, size = work[c]
        return jnp.dot(k_ref[0, start:start + size, :], qts[i], preferred_element_type=jnp.float32)

    def consume(c, s):
        i, _, vlo_ref, vhi_ref, start, size = work[c]
        p = sms[i].probs(s)
        sms[i].add(jnp.concatenate([
            jnp.dot(vlo_ref[0, :, start:start + size], p[:, :tq], preferred_element_type=jnp.float32),
            jnp.dot(vhi_ref[0, :, start:start + size], p[:, tq:], preferred_element_type=jnp.float32)], axis=1))

    _software_pipeline(len(work), scores, consume)
    lam = lam_ref[...]
    lam_full = (jnp.exp(jnp.sum(lam[0:1] * lam[1:2], axis=-1, keepdims=True))
                - jnp.exp(jnp.sum(lam[2:3] * lam[3:4], axis=-1, keepdims=True)) + lam_init)
    o = sms[0].result() - lam_full * sms[1].result()
    o = o * lax.rsqrt(jnp.mean(o * o, axis=0, keepdims=True) + EPS) * g_ref[...] * (1.0 - lam_init)
    o_ref[0, :, :LANES] = o[:, :tq].T.astype(o_ref.dtype)
    o_ref[0, :, LANES:] = o[:, tq:].T.astype(o_ref.dtype)


def _diff_attention(q_arr, sources, lam, subln_g, lam_init, *, tq):
    bsz, t, _ = q_arr.shape
    n_pairs = 4
    in_specs = [
        pl.BlockSpec((1, tq, LANES), lambda b, p, i: (b, i, p)),
        pl.BlockSpec((1, tq, LANES), lambda b, p, i: (b, i, n_pairs + p)),
        pl.BlockSpec(lam.shape, lambda b, p, i: (0, 0)),
        pl.BlockSpec(subln_g.shape, lambda b, p, i: (0, 0)),
    ]
    args = [q_arr, q_arr, lam, subln_g]
    for k_arr, vt_arr in sources:
        tk = k_arr.shape[1]
        in_specs += [
            pl.BlockSpec((1, tk, LANES), lambda b, p, i: (b, 0, 2 * n_pairs + p)),
            pl.BlockSpec((1, tk, LANES), lambda b, p, i: (b, 0, 3 * n_pairs + p)),
            pl.BlockSpec((1, VT_ROWS, tk), lambda b, p, i: (b, 2 * p, 0)),
            pl.BlockSpec((1, VT_ROWS, tk), lambda b, p, i: (b, 2 * p + 1, 0)),
        ]
        args += [k_arr, k_arr, vt_arr, vt_arr]
    return pl.pallas_call(
        functools.partial(_diff_attn_kernel, n_src=len(sources), lam_init=lam_init),
        grid=(bsz, n_pairs, t // tq),
        in_specs=in_specs,
        out_specs=pl.BlockSpec((1, tq, 2 * LANES), lambda b, p, i: (b, i, p)),
        out_shape=jax.ShapeDtypeStruct((bsz, t, 2 * LANES * n_pairs), jnp.bfloat16),
        compiler_params=_params("parallel", "parallel", "arbitrary"),
        name="diff_attention",
    )(*args)


def _na_attn_kernel(q_ref, k_ref, v_ref, kc_ref, vct_ref, bias_ref, o_ref, *, n_rows):
    step = pl.program_id(2)
    tg = NA_GROUP_ROWS * GRID_W
    n_groups = NA_ROWS_PER_STEP // NA_GROUP_ROWS
    win_rows = NA_WIN_H + NA_GROUP_ROWS - 1
    half = NA_WIN_H // 2
    qts = [_split_pair_t(q_ref[0, g * tg:(g + 1) * tg, :]) for g in range(n_groups)]
    s_cx_all = jnp.dot(kc_ref[0], jnp.concatenate(qts, axis=1), preferred_element_type=jnp.float32)
    lane = lax.broadcasted_iota(jnp.int32, (1, 2 * tg), 1)
    first_row = (lane & GRID_W) == 0

    def window(g):
        r0 = step * NA_ROWS_PER_STEP + NA_GROUP_ROWS * g
        rs0 = jnp.clip(r0 - half, 0, n_rows - NA_WIN_H)
        rs1 = jnp.clip(r0 + 1 - half, 0, n_rows - NA_WIN_H)
        ws = jnp.minimum(rs0, n_rows - win_rows)
        return r0, rs0, rs1, ws, pl.multiple_of(ws * GRID_W, GRID_W)

    def scores(g):
        r0, rs0, rs1, ws, tok0 = window(g)
        pieces = []
        for kr in range(win_rows):
            key_row = ws + kr
            e = jnp.clip(key_row - r0 + NA_WIN_H - 1, 0, 2 * NA_WIN_H - 1)
            out0 = jnp.where((key_row >= rs0) & (key_row < rs0 + NA_WIN_H), 0.0, MASK_VALUE)
            out1 = jnp.where((key_row >= rs1) & (key_row < rs1 + NA_WIN_H), 0.0, MASK_VALUE)
            pieces.append(bias_ref[0, e] + jnp.where(first_row, out0, out1))
        k_nb = k_ref[0, pl.ds(tok0, win_rows * GRID_W), :]
        return jnp.dot(k_nb, qts[g], preferred_element_type=jnp.float32) + jnp.concatenate(pieces, axis=0)

    def consume(g, s_nb):
        tok0 = window(g)[-1]
        vt_nb = jnp.concatenate([v_ref[0, pl.ds(tok0, win_rows * GRID_W), :].T,
                                 jnp.ones((SUM_ROWS, win_rows * GRID_W), jnp.bfloat16)], axis=0)
        s_cx = s_cx_all[:, g * 2 * tg:(g + 1) * 2 * tg]
        m = jnp.maximum(jnp.max(s_nb, axis=0, keepdims=True), jnp.max(s_cx, axis=0, keepdims=True))
        p_nb = jnp.exp2(s_nb - m).astype(jnp.bfloat16)
        p_cx = jnp.exp2(s_cx - m).astype(jnp.bfloat16)
        o = (jnp.dot(vt_nb, p_nb, preferred_element_type=jnp.float32)
             + jnp.dot(vct_ref[0], p_cx, preferred_element_type=jnp.float32))
        o = o[:LANES] * (1.0 / o[LANES:LANES + 1])
        o_ref[0, g * tg:(g + 1) * tg, :] = _merge_pair_t(o).astype(o_ref.dtype)

    _software_pipeline(n_groups, scores, consume)


def _na_attention(qkv, qk_ctx, vt_ctx, bias):
    bsz, s, _ = qkv.shape
    l_ctx = qk_ctx.shape[1]
    n_pairs = bias.shape[0]
    tq = NA_ROWS_PER_STEP * GRID_W
    return pl.pallas_call(
        functools.partial(_na_attn_kernel, n_rows=s // GRID_W),
        grid=(bsz, n_pairs, s // tq),
        in_specs=[
            pl.BlockSpec((1, tq, LANES), lambda b, p, i: (b, i, p)),
            pl.BlockSpec((1, s, LANES), lambda b, p, i: (b, 0, n_pairs + p)),
            pl.BlockSpec((1, s, LANES), lambda b, p, i: (b, 0, 2 * n_pairs + p)),
            pl.BlockSpec((1, l_ctx, LANES), lambda b, p, i: (b, 0, n_pairs + p)),
            pl.BlockSpec((1, VT_ROWS, l_ctx), lambda b, p, i: (b, p, 0)),
            pl.BlockSpec((1,) + bias.shape[1:], lambda b, p, i: (p, 0, 0, 0)),
        ],
        out_specs=pl.BlockSpec((1, tq, LANES), lambda b, p, i: (b, i, p)),
        out_shape=jax.ShapeDtypeStruct((bsz, s, n_pairs * LANES), jnp.bfloat16),
        compiler_params=_params("parallel", "parallel", "arbitrary"),
        name="na_attention",
    )(qkv, qkv, qkv, qk_ctx, vt_ctx, bias)


def _rope_tables(n):
    t = jnp.arange(n, dtype=jnp.int32)
    row = (t // GRID_W).astype(jnp.float32)
    col = (t % GRID_W).astype(jnp.float32)
    nf = HEAD_DIM // 4
    inv = 1.0 / (ROPE_THETA ** (jnp.arange(nf, dtype=jnp.float32) / nf))
    ar = row[:, None] * inv
    ac = col[:, None] * inv
    zero = jnp.zeros_like(ar)
    cos = jnp.concatenate([jnp.cos(ar), jnp.cos(ar), jnp.cos(ac), jnp.cos(ac)], axis=-1)
    sin_up = jnp.concatenate([-jnp.sin(ar), zero, -jnp.sin(ac), zero], axis=-1)
    sin_down = jnp.concatenate([zero, jnp.sin(ar), zero, jnp.sin(ac)], axis=-1)
    return tuple(jnp.tile(x, (1, LANES // HEAD_DIM)) for x in (cos, sin_up, sin_down))


def _na_bias_table(rpb):
    n_rel = 2 * NA_WIN_H - 1
    cpos = jnp.arange(GRID_W, dtype=jnp.int32)
    cidx = jnp.clip(cpos[None, :] - cpos[:, None] + NA_WIN_W - 1, 0, 2 * NA_WIN_W - 2)
    cstart = jnp.clip(cpos - NA_WIN_W // 2, 0, GRID_W - NA_WIN_W)
    col_ok = (cpos[None, :] >= cstart[:, None]) & (cpos[None, :] < cstart[:, None] + NA_WIN_W)
    t = jnp.where(col_ok, rpb[:, :, cidx] * LOG2_E, MASK_VALUE)
    t = jnp.swapaxes(t, 2, 3)
    e = jnp.arange(n_rel + 1)
    both = jnp.concatenate([t[:, jnp.clip(e, 0, n_rel - 1)], t[:, jnp.clip(e - 1, 0, n_rel - 1)]], axis=-1)
    n_heads = both.shape[0]
    both = both.reshape(n_heads // 2, 2, n_rel + 1, GRID_W, 2 * GRID_W).transpose(0, 2, 3, 1, 4)
    return both.reshape(n_heads // 2, n_rel + 1, GRID_W, 4 * GRID_W)


def _dup_heads(w):
    d, n = w.shape
    return jnp.tile(w.reshape(d, n // HEAD_DIM, 1, HEAD_DIM), (1, 1, 2, 1)).reshape(d, 2 * n)


def _tile_gain(g):
    return jnp.tile(g, LANES // HEAD_DIM)


def kernel(x, c, ctx, c_ctx, w_mod, b_mod, norm_g, w_ffn_in, w_ffn_out, na_w_qkv, na_rpb, na_w_o, gqa_w_qkv, gqa_q_gain, gqa_k_gain, gqa_w_o, diff_w_qkv, diff_lam, diff_subln_g, diff_w_o):
    bsz, seq, d = x.shape
    l_ctx = ctx.shape[1]
    depth = w_mod.shape[0]
    bf16 = jnp.bfloat16
    tm = 512
    tm_ctx = l_ctx
    tq = 512
    n_pairs = d // LANES

    n_cond = 8 * ((bsz + 1 + 7) // 8)
    cond = jnp.concatenate([c, c_ctx[None], jnp.zeros((n_cond - bsz - 1, d), c.dtype)], axis=0)
    mod_all = _modulation(cond, w_mod, b_mod).reshape(depth, n_cond, N_MOD, d)
    ctx_row = bsz

    rope = _rope_tables(seq)
    no_rope = tuple(jnp.zeros((l_ctx, LANES), jnp.float32) for _ in range(3))
    no_gain = jnp.ones((2, LANES), jnp.float32)
    value_t = _Ep(token_major=False, vt_features=LANES)

    h, hc = x, ctx
    for i in range(depth):
        kind, j, last = i % 3, i // 3, i == depth - 1
        mod, g = mod_all[i], norm_g[i]
        win = w_ffn_in[i].astype(bf16)
        wout = w_ffn_out[i].astype(bf16)

        h = _ffn(h, mod, None, g, win[0], wout[0], which=0, tm=tm)
        hc = _ffn(hc, mod, ctx_row, g, win[0], wout[0], which=0, tm=tm_ctx)

        if kind == 0:
            w = na_w_qkv[j].astype(bf16)
            ep = [_Ep(qscale=True)] * n_pairs + [_Ep()] * (2 * n_pairs)
            ep_c = [_Ep(qscale=True)] * n_pairs + [_Ep()] * n_pairs + [value_t] * n_pairs
            qkv, _ = _project(h, mod, None, g, w, no_gain, rope, ep, tm=tm)
            qkv_c, vt_c = _project(hc, mod, ctx_row, g, w, no_gain, no_rope, ep_c, tm=tm_ctx)
            o = _na_attention(qkv, qkv_c, vt_c, _na_bias_table(na_rpb[j]))
            if not last:
                oc = _pair_attention(qkv_c, 0, [(qkv_c, n_pairs, vt_c)], n_pairs, lambda p: p, tq=l_ctx)
            w_o = na_w_o[j].astype(bf16)
        elif kind == 1:
            wf = gqa_w_qkv[j]
            dkv = (wf.shape[1] - d) // 2
            w = jnp.concatenate([wf[:, :d], _dup_heads(wf[:, d:d + dkv]), wf[:, d + dkv:]], axis=1).astype(bf16)
            gains = jnp.stack([_tile_gain(gqa_q_gain[j]), _tile_gain(gqa_k_gain[j])])
            n_kv = dkv // HEAD_DIM
            rep_pairs = n_pairs // n_kv
            shared_v = [_Ep(token_major=False, vt_features=HEAD_DIM)] * (dkv // LANES)
            ep = [_Ep(norm=0, rope=True, qscale=True)] * n_pairs + [_Ep(norm=1, rope=True)] * n_kv + shared_v
            ep_c = [_Ep(norm=0, qscale=True)] * n_pairs + [_Ep(norm=1)] * n_kv + shared_v
            qk, vt = _project(h, mod, None, g, w, gains, rope, ep, tm=tm)
            qk_c, vt_c = _project(hc, mod, ctx_row, g, w, gains, no_rope, ep_c, tm=tm_ctx)
            o = _pair_attention(qk, 0, [(qk, n_pairs, vt), (qk_c, n_pairs, vt_c)], n_pairs,
                                lambda p: p // rep_pairs, tq=tq, v_features=HEAD_DIM)
            if not last:
                oc = _pair_attention(qk_c, 0, [(qk_c, n_pairs, vt_c)], n_pairs, lambda p: p // rep_pairs,
                                     tq=l_ctx, v_features=HEAD_DIM)
            w_o = gqa_w_o[j].astype(bf16)
        else:
            w = diff_w_qkv[j].astype(bf16)
            lam_init = 0.8 - 0.6 * math.exp(-0.3 * i)
            ep = [_Ep(rope=True, qscale=True)] * n_pairs + [_Ep(rope=True)] * n_pairs + [value_t] * n_pairs
            ep_c = [_Ep(qscale=True)] * n_pairs + [_Ep()] * n_pairs + [value_t] * n_pairs
            qk, vt = _project(h, mod, None, g, w, no_gain, rope, ep, tm=tm)
            qk_c, vt_c = _project(hc, mod, ctx_row, g, w, no_gain, no_rope, ep_c, tm=tm_ctx)
            sub_g = diff_subln_g[j][:, None]
            o = _diff_attention(qk, [(qk, vt), (qk_c, vt_c)], diff_lam[j], sub_g, lam_init, tq=tq)
            if not last:
                oc = _diff_attention(qk_c, [(qk_c, vt_c)], diff_lam[j], sub_g, lam_init, tq=l_ctx)
            w_o = diff_w_o[j].astype(bf16)

        h = _ffn(h, mod, None, g, win[1], wout[1], which=1, tm=tm, attn=o, w_o=w_o)
        if not last:
            hc = _ffn(hc, mod, ctx_row, g, win[1], wout[1], which=1, tm=tm_ctx, attn=oc, w_o=w_o)
    return h
```

```python
import functools
import math
from typing import NamedTuple, Optional

import jax
import jax.numpy as jnp
import numpy as np
from jax import lax
from jax.experimental import pallas as pl
from jax.experimental.pallas import tpu as pltpu

HEAD_DIM = 64
LANES = 128
GRID_W = 64
NA_WIN_H = 8
NA_WIN_W = 16
ROPE_THETA = 10000.0
EPS = 1e-6
N_MOD = 9
MASK_VALUE = -1e30
LOG2_E = math.log2(math.e)
QUERY_SCALE = HEAD_DIM ** -0.5 * LOG2_E
SUM_ROWS = 16
VT_ROWS = LANES + SUM_ROWS
VMEM_LIMIT_BYTES = 56 * 1024 * 1024

FFN_CHUNK = 256
PROJ_CHUNK = 256
KV_CHUNK = 256
SCORE_LOOKAHEAD = 5
NA_ROWS_PER_STEP = 32
NA_GROUP_ROWS = 2


def _params(*semantics):
    return pltpu.CompilerParams(dimension_semantics=semantics, vmem_limit_bytes=VMEM_LIMIT_BYTES)


def _resident(param):
    arr, lead = param
    rest = arr.shape[len(lead):]
    index = tuple(lead) + (0,) * len(rest)
    return pl.BlockSpec((None,) * len(lead) + rest, lambda *_: index, pipeline_mode=pl.Buffered(1))


def _rms(x, g):
    return x * lax.rsqrt(jnp.mean(x * x, axis=-1, keepdims=True) + EPS) * g


def _pre(x, g, shift, scale):
    return _rms(x, g) * (1.0 + scale) + shift


def _mod_kernel(c_ref, w_ref, b_ref, o_ref):
    c = c_ref[...]
    sc = (c * jax.nn.sigmoid(c)).astype(jnp.bfloat16)
    w = w_ref[0].astype(jnp.bfloat16)
    o_ref[0] = jnp.dot(sc, w, preferred_element_type=jnp.float32) + b_ref[0]


def _modulation(cond, w_mod, b_mod):
    depth, d, n = w_mod.shape
    r = cond.shape[0]
    tn = n // N_MOD
    return pl.pallas_call(
        _mod_kernel,
        grid=(depth, n // tn),
        in_specs=[
            pl.BlockSpec((r, d), lambda i, j: (0, 0)),
            pl.BlockSpec((1, d, tn), lambda i, j: (i, 0, j)),
            pl.BlockSpec((1, 1, tn), lambda i, j: (i, 0, j)),
        ],
        out_specs=pl.BlockSpec((1, r, tn), lambda i, j: (i, 0, j)),
        out_shape=jax.ShapeDtypeStruct((depth, r, n), jnp.float32),
        compiler_params=_params("arbitrary", "arbitrary"),
        name="modulation",
    )(cond, w_mod, b_mod.reshape(depth, 1, n))


def _ffn_kernel(x_ref, mod_ref, g_ref, win_ref, wout_ref, *rest, mod_row, g_row):
    o_ref, act_ref = rest[-2:]
    d_ff = wout_ref.shape[0]
    x = x_ref[0]
    if len(rest) == 4:
        attn_ref, wo_ref = rest[:2]
        y = jnp.dot(attn_ref[0], wo_ref[...], preferred_element_type=jnp.float32)
        x = x + mod_ref[0, 5:6, :] * _rms(y, g_ref[3:4, :])
    shift = mod_ref[0, mod_row:mod_row + 1, :]
    scale = mod_ref[0, mod_row + 1:mod_row + 2, :]
    gate = mod_ref[0, mod_row + 2:mod_row + 3, :]
    xm = _pre(x, g_ref[g_row:g_row + 1, :], shift, scale).astype(jnp.bfloat16)
    for j in range(d_ff // FFN_CHUNK):
        lo = j * FFN_CHUNK
        a = jnp.dot(xm, win_ref[:, lo:lo + FFN_CHUNK], preferred_element_type=jnp.float32)
        b = jnp.dot(xm, win_ref[:, d_ff + lo:d_ff + lo + FFN_CHUNK], preferred_element_type=jnp.float32)
        act_ref[:, lo:lo + FFN_CHUNK] = (a * jax.nn.sigmoid(a) * b).astype(jnp.bfloat16)
    y = jnp.dot(act_ref[...], wout_ref[...], preferred_element_type=jnp.float32)
    o_ref[0] = x + 0.5 * gate * _rms(y, g_ref[g_row + 1:g_row + 2, :])


def _mod_spec(mod, cond_row):
    arr, (layer,) = mod
    block = (None, 1) + arr.shape[2:]
    if cond_row is None:
        return pl.BlockSpec(block, lambda b, i: (layer, b, 0, 0))
    return pl.BlockSpec(block, lambda b, i: (layer, cond_row, 0, 0))


def _ffn(h, mod, cond_row, g, win, wout, *, which, tm, attn=None, w_o=None):
    bsz, t, d = h.shape
    d_ff = wout[0].shape[-2]
    kern = functools.partial(_ffn_kernel, mod_row=6 * which, g_row=4 * which)
    in_specs = [
        pl.BlockSpec((1, tm, d), lambda b, i: (b, i, 0)),
        _mod_spec(mod, cond_row),
        _resident(g),
        _resident(win),
        _resident(wout),
    ]
    args = [h, mod[0], g[0], win[0], wout[0]]
    if attn is not None:
        in_specs += [pl.BlockSpec((1, tm, d), lambda b, i: (b, i, 0)), _resident(w_o)]
        args += [attn, w_o[0]]
    return pl.pallas_call(
        kern,
        grid=(bsz, t // tm),
        in_specs=in_specs,
        out_specs=pl.BlockSpec((1, tm, d), lambda b, i: (b, i, 0)),
        out_shape=jax.ShapeDtypeStruct(h.shape, h.dtype),
        scratch_shapes=[pltpu.VMEM((tm, d_ff), jnp.bfloat16)],
        compiler_params=_params("parallel", "parallel"),
        name=f"ffn{which}",
    )(*args)


class _Ep(NamedTuple):
    norm: Optional[int] = None
    rope: bool = False
    qscale: bool = False
    token_major: bool = True
    feature_major: bool = False


def _head_rms(y, gain):
    lane = lax.broadcasted_iota(jnp.int32, y.shape, 1)
    low = lane < HEAD_DIM
    y2 = y * y
    ss_low = jnp.sum(jnp.where(low, y2, 0.0), axis=-1, keepdims=True)
    ss_high = jnp.sum(jnp.where(low, 0.0, y2), axis=-1, keepdims=True)
    r = jnp.where(low, lax.rsqrt(ss_low / HEAD_DIM + EPS), lax.rsqrt(ss_high / HEAD_DIM + EPS))
    return y * r * gain


def _rotary(y, cos, sin_up, sin_down):
    up = pltpu.roll(y, LANES - HEAD_DIM // 4, axis=1)
    down = pltpu.roll(y, HEAD_DIM // 4, axis=1)
    return y * cos + up * sin_up + down * sin_down


def _proj_kernel(x_ref, mod_ref, g_ref, w_ref, gain_ref, cos_ref, sup_ref, sdn_ref, *out_refs, epilogues):
    o_ref = out_refs[0]
    x = x_ref[0]
    shift = mod_ref[0, 3:4, :]
    scale = mod_ref[0, 4:5, :]
    xm = _pre(x, g_ref[2:3, :], shift, scale).astype(jnp.bfloat16)
    n_blocks = len(epilogues)
    per_dot = PROJ_CHUNK // LANES
    main_at = 0
    vt_row = 0
    for c in range(0, n_blocks, per_dot):
        width = min(per_dot, n_blocks - c) * LANES
        y_all = jnp.dot(xm, w_ref[:, c * LANES:c * LANES + width], preferred_element_type=jnp.float32)
        for s in range(width // LANES):
            ep = epilogues[c + s]
            y = y_all[:, s * LANES:(s + 1) * LANES]
            if ep.norm is not None:
                y = _head_rms(y, gain_ref[ep.norm:ep.norm + 1, :])
            if ep.rope:
                y = _rotary(y, cos_ref[...], sup_ref[...], sdn_ref[...])
            if ep.qscale:
                y = y * QUERY_SCALE
            if ep.token_major:
                o_ref[0, :, main_at * LANES:(main_at + 1) * LANES] = y.astype(o_ref.dtype)
                main_at += 1
            if ep.feature_major:
                vt_ref = out_refs[1]
                vt_ref[0, vt_row:vt_row + LANES, :] = y.T.astype(vt_ref.dtype)
                vt_ref[0, vt_row + LANES:vt_row + VT_ROWS, :] = jnp.ones((SUM_ROWS, y.shape[0]), vt_ref.dtype)
                vt_row += VT_ROWS


def _project(h, mod, cond_row, g, w, gains, rope, epilogues, *, tm):
    bsz, t, d = h.shape
    n_main = sum(ep.token_major for ep in epilogues)
    vt_rows = VT_ROWS * sum(ep.feature_major for ep in epilogues)
    cos, sin_up, sin_down = rope
    rope_spec = pl.BlockSpec((tm, LANES), lambda b, i: (i, 0))
    out_specs = [pl.BlockSpec((1, tm, n_main * LANES), lambda b, i: (b, i, 0))]
    out_shape = [jax.ShapeDtypeStruct((bsz, t, n_main * LANES), jnp.bfloat16)]
    if vt_rows:
        out_specs.append(pl.BlockSpec((1, vt_rows, tm), lambda b, i: (b, 0, i)))
        out_shape.append(jax.ShapeDtypeStruct((bsz, vt_rows, t), jnp.bfloat16))
    outs = pl.pallas_call(
        functools.partial(_proj_kernel, epilogues=tuple(epilogues)),
        grid=(bsz, t // tm),
        in_specs=[
            pl.BlockSpec((1, tm, d), lambda b, i: (b, i, 0)),
            _mod_spec(mod, cond_row),
            _resident(g),
            _resident(w),
            _resident((gains, ())),
            rope_spec, rope_spec, rope_spec,
        ],
        out_specs=out_specs,
        out_shape=out_shape,
        compiler_params=_params("parallel", "parallel"),
        name="qkv_proj",
    )(h, mod[0], g[0], w[0], gains, cos, sin_up, sin_down)
    return outs if vt_rows else (outs[0], None)


def _split_pair(q):
    lane = lax.broadcasted_iota(jnp.int32, q.shape, 1)
    zero = jnp.zeros_like(q)
    return jnp.concatenate([jnp.where(lane < HEAD_DIM, q, zero), jnp.where(lane < HEAD_DIM, zero, q)], axis=0)


def _split_pair_t(q):
    return _split_pair(q).T


def _merge_pair_t(o):
    tq = o.shape[1] // 2
    return jnp.concatenate([o[:HEAD_DIM, :tq], o[HEAD_DIM:, tq:]], axis=0).T


class _OnlineSoftmax:
    def __init__(self, n_queries):
        self.m = jnp.full((1, n_queries), -jnp.inf, jnp.float32)
        self.acc = jnp.zeros((VT_ROWS, n_queries), jnp.float32)

    def probs(self, s):
        m_new = jnp.maximum(self.m, jnp.max(s, axis=0, keepdims=True))
        self.acc = self.acc * jnp.exp2(self.m - m_new)
        self.m = m_new
        return jnp.exp2(s - m_new).astype(jnp.bfloat16)

    def add(self, pv):
        self.acc = self.acc + pv

    def result(self):
        return self.acc[:LANES] * (1.0 / self.acc[LANES:LANES + 1])


def _chunks(length):
    return [(start, min(KV_CHUNK, length - start)) for start in range(0, length, KV_CHUNK)]


def _software_pipeline(n, produce, consume):
    pending = {}
    for c in range(n + SCORE_LOOKAHEAD):
        if c < n:
            pending[c] = produce(c)
        if c >= SCORE_LOOKAHEAD:
            consume(c - SCORE_LOOKAHEAD, pending.pop(c - SCORE_LOOKAHEAD))


def _pair_attn_kernel(*refs, n_src):
    q_ref = refs[0]
    kv_refs = refs[1:1 + 2 * n_src]
    o_ref = refs[1 + 2 * n_src]
    qt = _split_pair_t(q_ref[0])
    sm = _OnlineSoftmax(qt.shape[1])
    work = [(kv_refs[2 * s], kv_refs[2 * s + 1], start, size)
            for s in range(n_src) for start, size in _chunks(kv_refs[2 * s].shape[1])]

    def scores(c):
        k_ref, _, start, size = work[c]
        return jnp.dot(k_ref[0, start:start + size, :], qt, preferred_element_type=jnp.float32)

    def consume(c, s):
        _, vt_ref, start, size = work[c]
        p = sm.probs(s)
        sm.add(jnp.dot(vt_ref[0, :, start:start + size], p, preferred_element_type=jnp.float32))

    _software_pipeline(len(work), scores, consume)
    o_ref[0] = _merge_pair_t(sm.result()).astype(o_ref.dtype)


def _pair_attention(q_arr, q_block0, sources, n_pairs, k_of_pair, *, tq):
    bsz, t, _ = q_arr.shape
    in_specs = [pl.BlockSpec((1, tq, LANES), lambda b, p, i: (b, i, q_block0 + p))]
    args = [q_arr]
    for k_arr, kb, vt_arr in sources:
        tk = k_arr.shape[1]
        in_specs.append(pl.BlockSpec((1, tk, LANES), lambda b, p, i, kb=kb: (b, 0, kb + k_of_pair(p))))
        in_specs.append(pl.BlockSpec((1, VT_ROWS, tk), lambda b, p, i: (b, k_of_pair(p), 0)))
        args += [k_arr, vt_arr]
    return pl.pallas_call(
        functools.partial(_pair_attn_kernel, n_src=len(sources)),
        grid=(bsz, n_pairs, t // tq),
        in_specs=in_specs,
        out_specs=pl.BlockSpec((1, tq, LANES), lambda b, p, i: (b, i, p)),
        out_shape=jax.ShapeDtypeStruct((bsz, t, n_pairs * LANES), jnp.bfloat16),
        compiler_params=_params("parallel", "parallel", "arbitrary"),
        name="pair_attention",
    )(*args)


def _diff_attn_kernel(*refs, n_src, lam_init):
    q1_ref, q2_ref, lam_ref, g_ref = refs[:4]
    kv_refs = refs[4:4 + 4 * n_src]
    o_ref = refs[4 + 4 * n_src]
    tq = q1_ref.shape[1]
    qts = (_split_pair_t(q1_ref[0]), _split_pair_t(q2_ref[0]))
    sms = (_OnlineSoftmax(2 * tq), _OnlineSoftmax(2 * tq))
    work = [(i, kv_refs[4 * s + i], kv_refs[4 * s + 2], kv_refs[4 * s + 3], start, size)
            for s in range(n_src) for start, size in _chunks(kv_refs[4 * s].shape[1]) for i in range(2)]

    def scores(c):
        i, k_ref, _, _, start, size = work[c]
        return jnp.dot(k_ref[0, start:start + size, :], qts[i], preferred_element_type=jnp.float32)

    def consume(c, s):
        i, _, vlo_ref, vhi_ref, start, size = work[c]
        p = sms[i].probs(s)
        sms[i].add(jnp.concatenate([
            jnp.dot(vlo_ref[0, :, start:start + size], p[:, :tq], preferred_element_type=jnp.float32),
            jnp.dot(vhi_ref[0, :, start:start + size], p[:, tq:], preferred_element_type=jnp.float32)], axis=1))

    _software_pipeline(len(work), scores, consume)
    lam = lam_ref[...]
    lam_full = (jnp.exp(jnp.sum(lam[0:1] * lam[1:2], axis=-1, keepdims=True))
                - jnp.exp(jnp.sum(lam[2:3] * lam[3:4], axis=-1, keepdims=True)) + lam_init)
    o = sms[0].result() - lam_full * sms[1].result()
    o = o * lax.rsqrt(jnp.mean(o * o, axis=0, keepdims=True) + EPS) * g_ref[...] * (1.0 - lam_init)
    o_ref[0, :, :LANES] = o[:, :tq].T.astype(o_ref.dtype)
    o_ref[0, :, LANES:] = o[:, tq:].T.astype(o_ref.dtype)


def _diff_attention(q_arr, sources, lam, subln_g, lam_init, *, tq):
    bsz, t, _ = q_arr.shape
    n_pairs = 4
    in_specs = [
        pl.BlockSpec((1, tq, LANES), lambda b, p, i: (b, i, p)),
        pl.BlockSpec((1, tq, LANES), lambda b, p, i: (b, i, n_pairs + p)),
        pl.BlockSpec(lam.shape, lambda b, p, i: (0, 0)),
        pl.BlockSpec(subln_g.shape, lambda b, p, i: (0, 0)),
    ]
    args = [q_arr, q_arr, lam, subln_g]
    for k_arr, vt_arr in sources:
        tk = k_arr.shape[1]
        in_specs += [
            pl.BlockSpec((1, tk, LANES), lambda b, p, i: (b, 0, 2 * n_pairs + p)),
            pl.BlockSpec((1, tk, LANES), lambda b, p, i: (b, 0, 3 * n_pairs + p)),
            pl.BlockSpec((1, VT_ROWS, tk), lambda b, p, i: (b, 2 * p, 0)),
            pl.BlockSpec((1, VT_ROWS, tk), lambda b, p, i: (b, 2 * p + 1, 0)),
        ]
        args += [k_arr, k_arr, vt_arr, vt_arr]
    return pl.pallas_call(
        functools.partial(_diff_attn_kernel, n_src=len(sources), lam_init=lam_init),
        grid=(bsz, n_pairs, t // tq),
        in_specs=in_specs,
        out_specs=pl.BlockSpec((1, tq, 2 * LANES), lambda b, p, i: (b, i, p)),
        out_shape=jax.ShapeDtypeStruct((bsz, t, 2 * LANES * n_pairs), jnp.bfloat16),
        compiler_params=_params("parallel", "parallel", "arbitrary"),
        name="diff_attention",
    )(*args)


def _na_attn_kernel(q_ref, k_ref, v_ref, kc_ref, vct_ref, bias_ref, o_ref, *, n_rows):
    step = pl.program_id(2)
    tg = NA_GROUP_ROWS * GRID_W
    n_groups = NA_ROWS_PER_STEP // NA_GROUP_ROWS
    win_rows = NA_WIN_H + NA_GROUP_ROWS - 1
    half = NA_WIN_H // 2
    qts = [_split_pair_t(q_ref[0, g * tg:(g + 1) * tg, :]) for g in range(n_groups)]
    s_cx_all = jnp.dot(kc_ref[0], jnp.concatenate(qts, axis=1), preferred_element_type=jnp.float32)
    lane = lax.broadcasted_iota(jnp.int32, (1, 2 * tg), 1)
    first_row = (lane & GRID_W) == 0

    def window(g):
        r0 = step * NA_ROWS_PER_STEP + NA_GROUP_ROWS * g
        rs0 = jnp.clip(r0 - half, 0, n_rows - NA_WIN_H)
        rs1 = jnp.clip(r0 + 1 - half, 0, n_rows - NA_WIN_H)
        ws = jnp.minimum(rs0, n_rows - win_rows)
        return r0, rs0, rs1, ws, pl.multiple_of(ws * GRID_W, GRID_W)

    def scores(g):
        r0, rs0, rs1, ws, tok0 = window(g)
        pieces = []
        for kr in range(win_rows):
            key_row = ws + kr
            e = jnp.clip(key_row - r0 + NA_WIN_H - 1, 0, 2 * NA_WIN_H - 1)
            out0 = jnp.where((key_row >= rs0) & (key_row < rs0 + NA_WIN_H), 0.0, MASK_VALUE)
            out1 = jnp.where((key_row >= rs1) & (key_row < rs1 + NA_WIN_H), 0.0, MASK_VALUE)
            pieces.append(bias_ref[0, e] + jnp.where(first_row, out0, out1))
        k_nb = k_ref[0, pl.ds(tok0, win_rows * GRID_W), :]
        return jnp.dot(k_nb, qts[g], preferred_element_type=jnp.float32) + jnp.concatenate(pieces, axis=0)

    def consume(g, s_nb):
        tok0 = window(g)[-1]
        vt_nb = jnp.concatenate([v_ref[0, pl.ds(tok0, win_rows * GRID_W), :].T,
                                 jnp.ones((SUM_ROWS, win_rows * GRID_W), jnp.bfloat16)], axis=0)
        s_cx = s_cx_all[:, g * 2 * tg:(g + 1) * 2 * tg]
        m = jnp.maximum(jnp.max(s_nb, axis=0, keepdims=True), jnp.max(s_cx, axis=0, keepdims=True))
        p_nb = jnp.exp2(s_nb - m).astype(jnp.bfloat16)
        p_cx = jnp.exp2(s_cx - m).astype(jnp.bfloat16)
        o = (jnp.dot(vt_nb, p_nb, preferred_element_type=jnp.float32)
             + jnp.dot(vct_ref[0], p_cx, preferred_element_type=jnp.float32))
        o = o[:LANES] * (1.0 / o[LANES:LANES + 1])
        o_ref[0, g * tg:(g + 1) * tg, :] = _merge_pair_t(o).astype(o_ref.dtype)

    _software_pipeline(n_groups, scores, consume)


def _na_attention(qkv, qk_ctx, vt_ctx, bias):
    bsz, s, _ = qkv.shape
    l_ctx = qk_ctx.shape[1]
    n_pairs = bias.shape[0]
    tq = NA_ROWS_PER_STEP * GRID_W
    return pl.pallas_call(
        functools.partial(_na_attn_kernel, n_rows=s // GRID_W),
        grid=(bsz, n_pairs, s // tq),
        in_specs=[
            pl.BlockSpec((1, tq, LANES), lambda b, p, i: (b, i, p)),
            pl.BlockSpec((1, s, LANES), lambda b, p, i: (b, 0, n_pairs + p)),
            pl.BlockSpec((1, s, LANES), lambda b, p, i: (b, 0, 2 * n_pairs + p)),
            pl.BlockSpec((1, l_ctx, LANES), lambda b, p, i: (b, 0, n_pairs + p)),
            pl.BlockSpec((1, VT_ROWS, l_ctx), lambda b, p, i: (b, p, 0)),
            pl.BlockSpec((1,) + bias.shape[1:], lambda b, p, i: (p, 0, 0, 0)),
        ],
        out_specs=pl.BlockSpec((1, tq, LANES), lambda b, p, i: (b, i, p)),
        out_shape=jax.ShapeDtypeStruct((bsz, s, n_pairs * LANES), jnp.bfloat16),
        compiler_params=_params("parallel", "parallel", "arbitrary"),
        name="na_attention",
    )(qkv, qkv, qkv, qk_ctx, vt_ctx, bias)


def _rope_tables(n):
    t = jnp.arange(n, dtype=jnp.int32)
    row = (t // GRID_W).astype(jnp.float32)
    col = (t % GRID_W).astype(jnp.float32)
    nf = HEAD_DIM // 4
    inv = 1.0 / (ROPE_THETA ** (jnp.arange(nf, dtype=jnp.float32) / nf))
    ar = row[:, None] * inv
    ac = col[:, None] * inv
    zero = jnp.zeros_like(ar)
    cos = jnp.concatenate([jnp.cos(ar), jnp.cos(ar), jnp.cos(ac), jnp.cos(ac)], axis=-1)
    sin_up = jnp.concatenate([-jnp.sin(ar), zero, -jnp.sin(ac), zero], axis=-1)
    sin_down = jnp.concatenate([zero, jnp.sin(ar), zero, jnp.sin(ac)], axis=-1)
    return tuple(jnp.tile(x, (1, LANES // HEAD_DIM)) for x in (cos, sin_up, sin_down))


def _na_bias_table(rpb):
    n_heads, n_rel, _ = rpb.shape
    lane = np.arange(4 * GRID_W)
    head_in_pair, second_row, qcol = lane // (2 * GRID_W), (lane // GRID_W) % 2, lane % GRID_W
    head = 2 * np.arange(n_heads // 2)[:, None] + head_in_pair[None, :]
    rel = np.clip(np.arange(n_rel + 1)[:, None] - second_row[None, :], 0, n_rel - 1)
    kcol = np.arange(GRID_W)[:, None]
    cidx = np.clip(kcol - qcol[None, :] + NA_WIN_W - 1, 0, 2 * NA_WIN_W - 2)
    cstart = np.clip(qcol - NA_WIN_W // 2, 0, GRID_W - NA_WIN_W)
    col_ok = (kcol >= cstart[None, :]) & (kcol < cstart[None, :] + NA_WIN_W)
    bias = rpb[head[:, None, None, :], rel[None, :, None, :], cidx[None, None, :, :]]
    return jnp.where(col_ok[None, None], bias * LOG2_E, MASK_VALUE)


def _dup_heads(w):
    d, n = w.shape
    return jnp.tile(w.reshape(d, n // HEAD_DIM, 1, HEAD_DIM), (1, 1, 2, 1)).reshape(d, 2 * n)


def _tile_gain(g):
    return jnp.tile(g, LANES // HEAD_DIM)


def kernel(x, c, ctx, c_ctx, w_mod, b_mod, norm_g, w_ffn_in, w_ffn_out, na_w_qkv, na_rpb, na_w_o, gqa_w_qkv, gqa_q_gain, gqa_k_gain, gqa_w_o, diff_w_qkv, diff_lam, diff_subln_g, diff_w_o):
    bsz, seq, d = x.shape
    l_ctx = ctx.shape[1]
    depth = w_mod.shape[0]
    bf16 = jnp.bfloat16
    tm = 512
    tm_ctx = l_ctx
    tq = 512
    n_pairs = d // LANES

    n_cond = 8 * ((bsz + 1 + 7) // 8)
    cond = jnp.concatenate([c, c_ctx[None], jnp.zeros((n_cond - bsz - 1, d), c.dtype)], axis=0)
    mod_all = _modulation(cond, w_mod, b_mod).reshape(depth, n_cond, N_MOD, d)
    ctx_row = bsz

    rope = _rope_tables(seq)
    no_rope = tuple(jnp.zeros((l_ctx, LANES), jnp.float32) for _ in range(3))
    no_gain = jnp.ones((2, LANES), jnp.float32)
    value_t = _Ep(token_major=False, feature_major=True)

    w_in_all, w_out_all = w_ffn_in.astype(bf16), w_ffn_out.astype(bf16)
    na_w_all, na_wo_all = na_w_qkv.astype(bf16), na_w_o.astype(bf16)
    gqa_wo_all = gqa_w_o.astype(bf16)
    diff_w_all, diff_wo_all = diff_w_qkv.astype(bf16), diff_w_o.astype(bf16)

    h, hc = x, ctx
    for i in range(depth):
        kind, j, last = i % 3, i // 3, i == depth - 1
        mod, g = (mod_all, (i,)), (norm_g, (i,))
        win, wout = [(w_in_all, (i, k)) for k in range(2)], [(w_out_all, (i, k)) for k in range(2)]

        h = _ffn(h, mod, None, g, win[0], wout[0], which=0, tm=tm)
        hc = _ffn(hc, mod, ctx_row, g, win[0], wout[0], which=0, tm=tm_ctx)

        if kind == 0:
            w = (na_w_all, (j,))
            ep = [_Ep(qscale=True)] * n_pairs + [_Ep()] * (2 * n_pairs)
            ep_c = [_Ep(qscale=True)] * n_pairs + [_Ep()] * n_pairs + [value_t] * n_pairs
            qkv, _ = _project(h, mod, None, g, w, no_gain, rope, ep, tm=tm)
            qkv_c, vt_c = _project(hc, mod, ctx_row, g, w, no_gain, no_rope, ep_c, tm=tm_ctx)
            o = _na_attention(qkv, qkv_c, vt_c, _na_bias_table(na_rpb[j]))
            if not last:
                oc = _pair_attention(qkv_c, 0, [(qkv_c, n_pairs, vt_c)], n_pairs, lambda p: p, tq=l_ctx)
            w_o = (na_wo_all, (j,))
        elif kind == 1:
            wf = gqa_w_qkv[j]
            dkv = (wf.shape[1] - d) // 2
            w = (jnp.concatenate([wf[:, :d], _dup_heads(wf[:, d:d + dkv]), _dup_heads(wf[:, d + dkv:])],
                                 axis=1).astype(bf16), ())
            gains = jnp.stack([_tile_gain(gqa_q_gain[j]), _tile_gain(gqa_k_gain[j])])
            n_kv = dkv // HEAD_DIM
            rep_pairs = n_pairs // n_kv
            ep = [_Ep(norm=0, rope=True, qscale=True)] * n_pairs + [_Ep(norm=1, rope=True)] * n_kv + [value_t] * n_kv
            ep_c = [_Ep(norm=0, qscale=True)] * n_pairs + [_Ep(norm=1)] * n_kv + [value_t] * n_kv
            qk, vt = _project(h, mod, None, g, w, gains, rope, ep, tm=tm)
            qk_c, vt_c = _project(hc, mod, ctx_row, g, w, gains, no_rope, ep_c, tm=tm_ctx)
            o = _pair_attention(qk, 0, [(qk, n_pairs, vt), (qk_c, n_pairs, vt_c)], n_pairs,
                                lambda p: p // rep_pairs, tq=tq)
            if not last:
                oc = _pair_attention(qk_c, 0, [(qk_c, n_pairs, vt_c)], n_pairs, lambda p: p // rep_pairs, tq=l_ctx)
            w_o = (gqa_wo_all, (j,))
        else:
            w = (diff_w_all, (j,))
            lam_init = 0.8 - 0.6 * math.exp(-0.3 * i)
            ep = [_Ep(rope=True, qscale=True)] * n_pairs + [_Ep(rope=True)] * n_pairs + [value_t] * n_pairs
            ep_c = [_Ep(qscale=True)] * n_pairs + [_Ep()] * n_pairs + [value_t] * n_pairs
            qk, vt = _project(h, mod, None, g, w, no_gain, rope, ep, tm=tm)
            qk_c, vt_c = _project(hc, mod, ctx_row, g, w, no_gain, no_rope, ep_c, tm=tm_ctx)
            sub_g = diff_subln_g[j][:, None]
            o = _diff_attention(qk, [(qk, vt), (qk_c, vt_c)], diff_lam[j], sub_g, lam_init, tq=tq)
            if not last:
                oc = _diff_attention(qk_c, [(qk_c, vt_c)], diff_lam[j], sub_g, lam_init, tq=l_ctx)
            w_o = (diff_wo_all, (j,))

        h = _ffn(h, mod, None, g, win[1], wout[1], which=1, tm=tm, attn=o, w_o=w_o)
        if not last:
            hc = _ffn(hc, mod, ctx_row, g, win[1], wout[1], which=1, tm=tm_ctx, attn=oc, w_o=w_o)
    return h
```

```python
import functools
import math
from typing import NamedTuple, Optional

import jax
import jax.numpy as jnp
import numpy as np
from jax import lax
from jax.experimental import pallas as pl
from jax.experimental.pallas import tpu as pltpu

HEAD_DIM = 64
LANES = 128
GRID_W = 64
NA_WIN_H = 8
NA_WIN_W = 16
ROPE_THETA = 10000.0
EPS = 1e-6
N_MOD = 9
MASK_VALUE = -1e30
LOG2_E = math.log2(math.e)
QUERY_SCALE = HEAD_DIM ** -0.5 * LOG2_E
SUM_ROWS = 16
VT_ROWS = LANES + SUM_ROWS
VMEM_LIMIT_BYTES = 56 * 1024 * 1024

FFN_CHUNK = 256
PROJ_CHUNK = 256
KV_CHUNK = 256
SCORE_LOOKAHEAD = 5
NA_ROWS_PER_STEP = 32
NA_GROUP_ROWS = 2


def _params(*semantics):
    return pltpu.CompilerParams(dimension_semantics=semantics, vmem_limit_bytes=VMEM_LIMIT_BYTES)


def _resident(param):
    arr, lead = param
    rest = arr.shape[len(lead):]
    index = tuple(lead) + (0,) * len(rest)
    return pl.BlockSpec((None,) * len(lead) + rest, lambda *_: index, pipeline_mode=pl.Buffered(1))


def _rms(x, g):
    return x * lax.rsqrt(jnp.mean(x * x, axis=-1, keepdims=True) + EPS) * g


def _pre(x, g, shift, scale):
    return _rms(x, g) * (1.0 + scale) + shift


def _mod_kernel(c_ref, w_ref, b_ref, o_ref):
    c = c_ref[...]
    sc = (c * jax.nn.sigmoid(c)).astype(jnp.bfloat16)
    w = w_ref[0].astype(jnp.bfloat16)
    o_ref[0] = jnp.dot(sc, w, preferred_element_type=jnp.float32) + b_ref[0]


def _modulation(cond, w_mod, b_mod):
    depth, d, n = w_mod.shape
    r = cond.shape[0]
    tn = n // N_MOD
    return pl.pallas_call(
        _mod_kernel,
        grid=(depth, n // tn),
        in_specs=[
            pl.BlockSpec((r, d), lambda i, j: (0, 0)),
            pl.BlockSpec((1, d, tn), lambda i, j: (i, 0, j)),
            pl.BlockSpec((1, 1, tn), lambda i, j: (i, 0, j)),
        ],
        out_specs=pl.BlockSpec((1, r, tn), lambda i, j: (i, 0, j)),
        out_shape=jax.ShapeDtypeStruct((depth, r, n), jnp.float32),
        compiler_params=_params("arbitrary", "arbitrary"),
        name="modulation",
    )(cond, w_mod, b_mod.reshape(depth, 1, n))


def _ffn_kernel(x_ref, mod_ref, g_ref, win_ref, wout_ref, *rest, mod_row, g_row):
    o_ref, act_ref = rest[-2:]
    d_ff = wout_ref.shape[0]
    x = x_ref[0]
    if len(rest) == 4:
        attn_ref, wo_ref = rest[:2]
        y = jnp.dot(attn_ref[0], wo_ref[...], preferred_element_type=jnp.float32)
        x = x + mod_ref[0, 5:6, :] * _rms(y, g_ref[3:4, :])
    shift = mod_ref[0, mod_row:mod_row + 1, :]
    scale = mod_ref[0, mod_row + 1:mod_row + 2, :]
    gate = mod_ref[0, mod_row + 2:mod_row + 3, :]
    xm = _pre(x, g_ref[g_row:g_row + 1, :], shift, scale).astype(jnp.bfloat16)
    for j in range(d_ff // FFN_CHUNK):
        lo = j * FFN_CHUNK
        a = jnp.dot(xm, win_ref[:, lo:lo + FFN_CHUNK], preferred_element_type=jnp.float32)
        b = jnp.dot(xm, win_ref[:, d_ff + lo:d_ff + lo + FFN_CHUNK], preferred_element_type=jnp.float32)
        act_ref[:, lo:lo + FFN_CHUNK] = (a * jax.nn.sigmoid(a) * b).astype(jnp.bfloat16)
    y = jnp.dot(act_ref[...], wout_ref[...], preferred_element_type=jnp.float32)
    o_ref[0] = x + 0.5 * gate * _rms(y, g_ref[g_row + 1:g_row + 2, :])


def _mod_spec(mod, cond_row):
    arr, (layer,) = mod
    block = (None, 1) + arr.shape[2:]
    if cond_row is None:
        return pl.BlockSpec(block, lambda b, i: (layer, b, 0, 0))
    return pl.BlockSpec(block, lambda b, i: (layer, cond_row, 0, 0))


def _ffn(h, mod, cond_row, g, win, wout, *, which, tm, attn=None, w_o=None):
    bsz, t, d = h.shape
    d_ff = wout[0].shape[-2]
    kern = functools.partial(_ffn_kernel, mod_row=6 * which, g_row=4 * which)
    in_specs = [
        pl.BlockSpec((1, tm, d), lambda b, i: (b, i, 0)),
        _mod_spec(mod, cond_row),
        _resident(g),
        _resident(win),
        _resident(wout),
    ]
    args = [h, mod[0], g[0], win[0], wout[0]]
    if attn is not None:
        in_specs += [pl.BlockSpec((1, tm, d), lambda b, i: (b, i, 0)), _resident(w_o)]
        args += [attn, w_o[0]]
    return pl.pallas_call(
        kern,
        grid=(bsz, t // tm),
        in_specs=in_specs,
        out_specs=pl.BlockSpec((1, tm, d), lambda b, i: (b, i, 0)),
        out_shape=jax.ShapeDtypeStruct(h.shape, h.dtype),
        scratch_shapes=[pltpu.VMEM((tm, d_ff), jnp.bfloat16)],
        compiler_params=_params("parallel", "parallel"),
        name=f"ffn{which}",
    )(*args)


class _Ep(NamedTuple):
    norm: Optional[int] = None
    rope: bool = False
    qscale: bool = False
    token_major: bool = True
    feature_major: bool = False


def _head_rms(y, gain):
    lane = lax.broadcasted_iota(jnp.int32, y.shape, 1)
    low = lane < HEAD_DIM
    y2 = y * y
    ss_low = jnp.sum(jnp.where(low, y2, 0.0), axis=-1, keepdims=True)
    ss_high = jnp.sum(jnp.where(low, 0.0, y2), axis=-1, keepdims=True)
    r = jnp.where(low, lax.rsqrt(ss_low / HEAD_DIM + EPS), lax.rsqrt(ss_high / HEAD_DIM + EPS))
    return y * r * gain


def _rotary(y, cos, sin_up, sin_down):
    up = pltpu.roll(y, LANES - HEAD_DIM // 4, axis=1)
    down = pltpu.roll(y, HEAD_DIM // 4, axis=1)
    return y * cos + up * sin_up + down * sin_down


def _proj_kernel(x_ref, mod_ref, g_ref, w_ref, gain_ref, cos_ref, sup_ref, sdn_ref, *out_refs, epilogues):
    o_ref = out_refs[0]
    x = x_ref[0]
    shift = mod_ref[0, 3:4, :]
    scale = mod_ref[0, 4:5, :]
    xm = _pre(x, g_ref[2:3, :], shift, scale).astype(jnp.bfloat16)
    n_blocks = len(epilogues)
    per_dot = PROJ_CHUNK // LANES
    main_at = 0
    vt_row = 0
    for c in range(0, n_blocks, per_dot):
        width = min(per_dot, n_blocks - c) * LANES
        y_all = jnp.dot(xm, w_ref[:, c * LANES:c * LANES + width], preferred_element_type=jnp.float32)
        for s in range(width // LANES):
            ep = epilogues[c + s]
            y = y_all[:, s * LANES:(s + 1) * LANES]
            if ep.norm is not None:
                y = _head_rms(y, gain_ref[ep.norm:ep.norm + 1, :])
            if ep.rope:
                y = _rotary(y, cos_ref[...], sup_ref[...], sdn_ref[...])
            if ep.qscale:
                y = y * QUERY_SCALE
            if ep.token_major:
                o_ref[0, :, main_at * LANES:(main_at + 1) * LANES] = y.astype(o_ref.dtype)
                main_at += 1
            if ep.feature_major:
                vt_ref = out_refs[1]
                vt_ref[0, vt_row:vt_row + LANES, :] = y.T.astype(vt_ref.dtype)
                vt_ref[0, vt_row + LANES:vt_row + VT_ROWS, :] = jnp.ones((SUM_ROWS, y.shape[0]), vt_ref.dtype)
                vt_row += VT_ROWS


def _project(h, mod, cond_row, g, w, gains, rope, epilogues, *, tm):
    bsz, t, d = h.shape
    n_main = sum(ep.token_major for ep in epilogues)
    vt_rows = VT_ROWS * sum(ep.feature_major for ep in epilogues)
    cos, sin_up, sin_down = rope
    rope_spec = pl.BlockSpec((tm, LANES), lambda b, i: (i, 0))
    out_specs = [pl.BlockSpec((1, tm, n_main * LANES), lambda b, i: (b, i, 0))]
    out_shape = [jax.ShapeDtypeStruct((bsz, t, n_main * LANES), jnp.bfloat16)]
    if vt_rows:
        out_specs.append(pl.BlockSpec((1, vt_rows, tm), lambda b, i: (b, 0, i)))
        out_shape.append(jax.ShapeDtypeStruct((bsz, vt_rows, t), jnp.bfloat16))
    outs = pl.pallas_call(
        functools.partial(_proj_kernel, epilogues=tuple(epilogues)),
        grid=(bsz, t // tm),
        in_specs=[
            pl.BlockSpec((1, tm, d), lambda b, i: (b, i, 0)),
            _mod_spec(mod, cond_row),
            _resident(g),
            _resident(w),
            _resident((gains, ())),
            rope_spec, rope_spec, rope_spec,
        ],
        out_specs=out_specs,
        out_shape=out_shape,
        compiler_params=_params("parallel", "parallel"),
        name="qkv_proj",
    )(h, mod[0], g[0], w[0], gains, cos, sin_up, sin_down)
    return outs if vt_rows else (outs[0], None)


def _split_pair(q):
    lane = lax.broadcasted_iota(jnp.int32, q.shape, 1)
    zero = jnp.zeros_like(q)
    return jnp.concatenate([jnp.where(lane < HEAD_DIM, q, zero), jnp.where(lane < HEAD_DIM, zero, q)], axis=0)


def _split_pair_t(q):
    return _split_pair(q).T


def _merge_pair_t(o):
    tq = o.shape[1] // 2
    return jnp.concatenate([o[:HEAD_DIM, :tq], o[HEAD_DIM:, tq:]], axis=0).T


class _OnlineSoftmax:
    def __init__(self, n_queries):
        self.m = jnp.full((1, n_queries), -jnp.inf, jnp.float32)
        self.acc = jnp.zeros((VT_ROWS, n_queries), jnp.float32)

    def probs(self, s):
        m_new = jnp.maximum(self.m, jnp.max(s, axis=0, keepdims=True))
        self.acc = self.acc * jnp.exp2(self.m - m_new)
        self.m = m_new
        return jnp.exp2(s - m_new).astype(jnp.bfloat16)

    def add(self, pv):
        self.acc = self.acc + pv

    def result(self):
        return self.acc[:LANES] * (1.0 / self.acc[LANES:LANES + 1])


def _chunks(length):
    return [(start, min(KV_CHUNK, length - start)) for start in range(0, length, KV_CHUNK)]


def _software_pipeline(n, produce, consume):
    pending = {}
    for c in range(n + SCORE_LOOKAHEAD):
        if c < n:
            pending[c] = produce(c)
        if c >= SCORE_LOOKAHEAD:
            consume(c - SCORE_LOOKAHEAD, pending.pop(c - SCORE_LOOKAHEAD))


def _pair_attn_kernel(*refs, n_src):
    q_ref = refs[0]
    kv_refs = refs[1:1 + 2 * n_src]
    o_ref = refs[1 + 2 * n_src]
    qt = _split_pair_t(q_ref[0])
    sm = _OnlineSoftmax(qt.shape[1])
    work = [(kv_refs[2 * s], kv_refs[2 * s + 1], start, size)
            for s in range(n_src) for start, size in _chunks(kv_refs[2 * s].shape[1])]

    def scores(c):
        k_ref, _, start, size = work[c]
        return jnp.dot(k_ref[0, start:start + size, :], qt, preferred_element_type=jnp.float32)

    def consume(c, s):
        _, vt_ref, start, size = work[c]
        p = sm.probs(s)
        sm.add(jnp.dot(vt_ref[0, :, start:start + size], p, preferred_element_type=jnp.float32))

    _software_pipeline(len(work), scores, consume)
    o_ref[0] = _merge_pair_t(sm.result()).astype(o_ref.dtype)


def _pair_attention(q_arr, q_block0, sources, n_pairs, k_of_pair, *, tq):
    bsz, t, _ = q_arr.shape
    in_specs = [pl.BlockSpec((1, tq, LANES), lambda b, p, i: (b, i, q_block0 + p))]
    args = [q_arr]
    for k_arr, kb, vt_arr in sources:
        tk = k_arr.shape[1]
        in_specs.append(pl.BlockSpec((1, tk, LANES), lambda b, p, i, kb=kb: (b, 0, kb + k_of_pair(p))))
        in_specs.append(pl.BlockSpec((1, VT_ROWS, tk), lambda b, p, i: (b, k_of_pair(p), 0)))
        args += [k_arr, vt_arr]
    return pl.pallas_call(
        functools.partial(_pair_attn_kernel, n_src=len(sources)),
        grid=(bsz, n_pairs, t // tq),
        in_specs=in_specs,
        out_specs=pl.BlockSpec((1, tq, LANES), lambda b, p, i: (b, i, p)),
        out_shape=jax.ShapeDtypeStruct((bsz, t, n_pairs * LANES), jnp.bfloat16),
        compiler_params=_params("parallel", "parallel", "arbitrary"),
        name="pair_attention",
    )(*args)


def _diff_attn_kernel(*refs, n_src, lam_init):
    q1_ref, q2_ref, lam_ref, g_ref = refs[:4]
    kv_refs = refs[4:4 + 4 * n_src]
    o_ref = refs[4 + 4 * n_src]
    tq = q1_ref.shape[1]
    qts = (_split_pair_t(q1_ref[0]), _split_pair_t(q2_ref[0]))
    sms = (_OnlineSoftmax(2 * tq), _OnlineSoftmax(2 * tq))
    work = [(i, kv_refs[4 * s + i], kv_refs[4 * s + 2], kv_refs[4 * s + 3], start, size)
            for s in range(n_src) for start, size in _chunks(kv_refs[4 * s].shape[1]) for i in range(2)]

    def scores(c):
        i, k_ref, _, _, start, size = work[c]
        return jnp.dot(k_ref[0, start:start + size, :], qts[i], preferred_element_type=jnp.float32)

    def consume(c, s):
        i, _, vlo_ref, vhi_ref, start, size = work[c]
        p = sms[i].probs(s)
        sms[i].add(jnp.concatenate([
            jnp.dot(vlo_ref[0, :, start:start + size], p[:, :tq], preferred_element_type=jnp.float32),
            jnp.dot(vhi_ref[0, :, start:start + size], p[:, tq:], preferred_element_type=jnp.float32)], axis=1))

    _software_pipeline(len(work), scores, consume)
    lam = lam_ref[...]
    lam_full = (jnp.exp(jnp.sum(lam[0:1] * lam[1:2], axis=-1, keepdims=True))
                - jnp.exp(jnp.sum(lam[2:3] * lam[3:4], axis=-1, keepdims=True)) + lam_init)
    o = sms[0].result() - lam_full * sms[1].result()
    o = o * lax.rsqrt(jnp.mean(o * o, axis=0, keepdims=True) + EPS) * g_ref[...] * (1.0 - lam_init)
    o_ref[0, :, :LANES] = o[:, :tq].T.astype(o_ref.dtype)
    o_ref[0, :, LANES:] = o[:, tq:].T.astype(o_ref.dtype)


def _diff_attention(q_arr, sources, lam, subln_g, lam_init, *, tq):
    bsz, t, _ = q_arr.shape
    n_pairs = 4
    in_specs = [
        pl.BlockSpec((1, tq, LANES), lambda b, p, i: (b, i, p)),
        pl.BlockSpec((1, tq, LANES), lambda b, p, i: (b, i, n_pairs + p)),
        pl.BlockSpec(lam.shape, lambda b, p, i: (0, 0)),
        pl.BlockSpec(subln_g.shape, lambda b, p, i: (0, 0)),
    ]
    args = [q_arr, q_arr, lam, subln_g]
    for k_arr, vt_arr in sources:
        tk = k_arr.shape[1]
        in_specs += [
            pl.BlockSpec((1, tk, LANES), lambda b, p, i: (b, 0, 2 * n_pairs + p)),
            pl.BlockSpec((1, tk, LANES), lambda b, p, i: (b, 0, 3 * n_pairs + p)),
            pl.BlockSpec((1, VT_ROWS, tk), lambda b, p, i: (b, 2 * p, 0)),
            pl.BlockSpec((1, VT_ROWS, tk), lambda b, p, i: (b, 2 * p + 1, 0)),
        ]
        args += [k_arr, k_arr, vt_arr, vt_arr]
    return pl.pallas_call(
        functools.partial(_diff_attn_kernel, n_src=len(sources), lam_init=lam_init),
        grid=(bsz, n_pairs, t // tq),
        in_specs=in_specs,
        out_specs=pl.BlockSpec((1, tq, 2 * LANES), lambda b, p, i: (b, i, p)),
        out_shape=jax.ShapeDtypeStruct((bsz, t, 2 * LANES * n_pairs), jnp.bfloat16),
        compiler_params=_params("parallel", "parallel", "arbitrary"),
        name="diff_attention",
    )(*args)


def _na_attn_kernel(q_ref, k_ref, v_ref, kc_ref, vct_ref, bias_ref, o_ref, *, n_rows):
    step = pl.program_id(2)
    tg = NA_GROUP_ROWS * GRID_W
    n_groups = NA_ROWS_PER_STEP // NA_GROUP_ROWS
    win_rows = NA_WIN_H + NA_GROUP_ROWS - 1
    half = NA_WIN_H // 2
    qts = [_split_pair_t(q_ref[0, g * tg:(g + 1) * tg, :]) for g in range(n_groups)]
    s_cx_all = jnp.dot(kc_ref[0], jnp.concatenate(qts, axis=1), preferred_element_type=jnp.float32)
    lane = lax.broadcasted_iota(jnp.int32, (1, 2 * tg), 1)
    first_row = (lane & GRID_W) == 0

    def window(g):
        r0 = step * NA_ROWS_PER_STEP + NA_GROUP_ROWS * g
        rs0 = jnp.clip(r0 - half, 0, n_rows - NA_WIN_H)
        rs1 = jnp.clip(r0 + 1 - half, 0, n_rows - NA_WIN_H)
        ws = jnp.minimum(rs0, n_rows - win_rows)
        return r0, rs0, rs1, ws, pl.multiple_of(ws * GRID_W, GRID_W)

    def scores(g):
        r0, rs0, rs1, ws, tok0 = window(g)
        pieces = []
        for kr in range(win_rows):
            key_row = ws + kr
            e = jnp.clip(key_row - r0 + NA_WIN_H - 1, 0, 2 * NA_WIN_H - 1)
            out0 = jnp.where((key_row >= rs0) & (key_row < rs0 + NA_WIN_H), 0.0, MASK_VALUE)
            out1 = jnp.where((key_row >= rs1) & (key_row < rs1 + NA_WIN_H), 0.0, MASK_VALUE)
            pieces.append(bias_ref[0, e] + jnp.where(first_row, out0, out1))
        k_nb = k_ref[0, pl.ds(tok0, win_rows * GRID_W), :]
        return jnp.dot(k_nb, qts[g], preferred_element_type=jnp.float32) + jnp.concatenate(pieces, axis=0)

    def consume(g, s_nb):
        tok0 = window(g)[-1]
        vt_nb = jnp.concatenate([v_ref[0, pl.ds(tok0, win_rows * GRID_W), :].T,
                                 jnp.ones((SUM_ROWS, win_rows * GRID_W), jnp.bfloat16)], axis=0)
        s_cx = s_cx_all[:, g * 2 * tg:(g + 1) * 2 * tg]
        m = jnp.maximum(jnp.max(s_nb, axis=0, keepdims=True), jnp.max(s_cx, axis=0, keepdims=True))
        p_nb = jnp.exp2(s_nb - m).astype(jnp.bfloat16)
        p_cx = jnp.exp2(s_cx - m).astype(jnp.bfloat16)
        o = (jnp.dot(vt_nb, p_nb, preferred_element_type=jnp.float32)
             + jnp.dot(vct_ref[0], p_cx, preferred_element_type=jnp.float32))
        o = o[:LANES] * (1.0 / o[LANES:LANES + 1])
        o_ref[0, g * tg:(g + 1) * tg, :] = _merge_pair_t(o).astype(o_ref.dtype)

    _software_pipeline(n_groups, scores, consume)


def _na_attention(qkv, qk_ctx, vt_ctx, bias):
    bsz, s, _ = qkv.shape
    l_ctx = qk_ctx.shape[1]
    n_pairs = bias.shape[0]
    tq = NA_ROWS_PER_STEP * GRID_W
    return pl.pallas_call(
        functools.partial(_na_attn_kernel, n_rows=s // GRID_W),
        grid=(bsz, n_pairs, s // tq),
        in_specs=[
            pl.BlockSpec((1, tq, LANES), lambda b, p, i: (b, i, p)),
            pl.BlockSpec((1, s, LANES), lambda b, p, i: (b, 0, n_pairs + p)),
            pl.BlockSpec((1, s, LANES), lambda b, p, i: (b, 0, 2 * n_pairs + p)),
            pl.BlockSpec((1, l_ctx, LANES), lambda b, p, i: (b, 0, n_pairs + p)),
            pl.BlockSpec((1, VT_ROWS, l_ctx), lambda b, p, i: (b, p, 0)),
            pl.BlockSpec((1,) + bias.shape[1:], lambda b, p, i: (p, 0, 0, 0)),
        ],
        out_specs=pl.BlockSpec((1, tq, LANES), lambda b, p, i: (b, i, p)),
        out_shape=jax.ShapeDtypeStruct((bsz, s, n_pairs * LANES), jnp.bfloat16),
        compiler_params=_params("parallel", "parallel", "arbitrary"),
        name="na_attention",
    )(qkv, qkv, qkv, qk_ctx, vt_ctx, bias)


def _rope_tables(n):
    t = jnp.arange(n, dtype=jnp.int32)
    row = (t // GRID_W).astype(jnp.float32)
    col = (t % GRID_W).astype(jnp.float32)
    nf = HEAD_DIM // 4
    inv = 1.0 / (ROPE_THETA ** (jnp.arange(nf, dtype=jnp.float32) / nf))
    ar = row[:, None] * inv
    ac = col[:, None] * inv
    zero = jnp.zeros_like(ar)
    cos = jnp.concatenate([jnp.cos(ar), jnp.cos(ar), jnp.cos(ac), jnp.cos(ac)], axis=-1)
    sin_up = jnp.concatenate([-jnp.sin(ar), zero, -jnp.sin(ac), zero], axis=-1)
    sin_down = jnp.concatenate([zero, jnp.sin(ar), zero, jnp.sin(ac)], axis=-1)
    return tuple(jnp.tile(x, (1, LANES // HEAD_DIM)) for x in (cos, sin_up, sin_down))


def _na_bias_table(rpb):
    n_heads, n_rel, _ = rpb.shape
    pad = GRID_W - NA_WIN_W
    w = jnp.pad(rpb, ((0, 0), (0, 0), (pad, pad)))
    t = jnp.stack([w[:, :, GRID_W - 1 - q:2 * GRID_W - 1 - q] for q in range(GRID_W)], axis=2)
    t = jnp.swapaxes(t, 2, 3)
    kcol, qcol = np.arange(GRID_W)[:, None], np.arange(GRID_W)[None, :]
    cstart = np.clip(qcol - NA_WIN_W // 2, 0, GRID_W - NA_WIN_W)
    col_ok = (kcol >= cstart) & (kcol < cstart + NA_WIN_W)
    t = jnp.where(col_ok, t * LOG2_E, MASK_VALUE)
    first = jnp.concatenate([t, t[:, -1:]], axis=1)
    second = jnp.concatenate([t[:, :1], t], axis=1)
    both = jnp.stack([first, second], axis=3)
    both = both.reshape(n_heads // 2, 2, n_rel + 1, GRID_W, 2 * GRID_W).transpose(0, 2, 3, 1, 4)
    return both.reshape(n_heads // 2, n_rel + 1, GRID_W, 4 * GRID_W)


def _dup_heads(w):
    d, n = w.shape
    return jnp.tile(w.reshape(d, n // HEAD_DIM, 1, HEAD_DIM), (1, 1, 2, 1)).reshape(d, 2 * n)


def _tile_gain(g):
    return jnp.tile(g, LANES // HEAD_DIM)


def kernel(x, c, ctx, c_ctx, w_mod, b_mod, norm_g, w_ffn_in, w_ffn_out, na_w_qkv, na_rpb, na_w_o, gqa_w_qkv, gqa_q_gain, gqa_k_gain, gqa_w_o, diff_w_qkv, diff_lam, diff_subln_g, diff_w_o):
    bsz, seq, d = x.shape
    l_ctx = ctx.shape[1]
    depth = w_mod.shape[0]
    bf16 = jnp.bfloat16
    tm = 512
    tm_ctx = l_ctx
    tq = 512
    n_pairs = d // LANES

    n_cond = 8 * ((bsz + 1 + 7) // 8)
    cond = jnp.concatenate([c, c_ctx[None], jnp.zeros((n_cond - bsz - 1, d), c.dtype)], axis=0)
    mod_all = _modulation(cond, w_mod, b_mod).reshape(depth, n_cond, N_MOD, d)
    ctx_row = bsz

    rope = _rope_tables(seq)
    no_rope = tuple(jnp.zeros((l_ctx, LANES), jnp.float32) for _ in range(3))
    no_gain = jnp.ones((2, LANES), jnp.float32)
    value_t = _Ep(token_major=False, feature_major=True)

    w_in_all, w_out_all = w_ffn_in.astype(bf16), w_ffn_out.astype(bf16)
    na_w_all, na_wo_all = na_w_qkv.astype(bf16), na_w_o.astype(bf16)
    gqa_wo_all = gqa_w_o.astype(bf16)
    diff_w_all, diff_wo_all = diff_w_qkv.astype(bf16), diff_w_o.astype(bf16)

    h, hc = x, ctx
    for i in range(depth):
        kind, j, last = i % 3, i // 3, i == depth - 1
        mod, g = (mod_all, (i,)), (norm_g, (i,))
        win, wout = [(w_in_all, (i, k)) for k in range(2)], [(w_out_all, (i, k)) for k in range(2)]

        h = _ffn(h, mod, None, g, win[0], wout[0], which=0, tm=tm)
        hc = _ffn(hc, mod, ctx_row, g, win[0], wout[0], which=0, tm=tm_ctx)

        if kind == 0:
            w = (na_w_all, (j,))
            ep = [_Ep(qscale=True)] * n_pairs + [_Ep()] * (2 * n_pairs)
            ep_c = [_Ep(qscale=True)] * n_pairs + [_Ep()] * n_pairs + [value_t] * n_pairs
            qkv, _ = _project(h, mod, None, g, w, no_gain, rope, ep, tm=tm)
            qkv_c, vt_c = _project(hc, mod, ctx_row, g, w, no_gain, no_rope, ep_c, tm=tm_ctx)
            o = _na_attention(qkv, qkv_c, vt_c, _na_bias_table(na_rpb[j]))
            if not last:
                oc = _pair_attention(qkv_c, 0, [(qkv_c, n_pairs, vt_c)], n_pairs, lambda p: p, tq=l_ctx)
            w_o = (na_wo_all, (j,))
        elif kind == 1:
            wf = gqa_w_qkv[j]
            dkv = (wf.shape[1] - d) // 2
            w = (jnp.concatenate([wf[:, :d], _dup_heads(wf[:, d:d + dkv]), _dup_heads(wf[:, d + dkv:])],
                                 axis=1).astype(bf16), ())
            gains = jnp.stack([_tile_gain(gqa_q_gain[j]), _tile_gain(gqa_k_gain[j])])
            n_kv = dkv // HEAD_DIM
            rep_pairs = n_pairs // n_kv
            ep = [_Ep(norm=0, rope=True, qscale=True)] * n_pairs + [_Ep(norm=1, rope=True)] * n_kv + [value_t] * n_kv
            ep_c = [_Ep(norm=0, qscale=True)] * n_pairs + [_Ep(norm=1)] * n_kv + [value_t] * n_kv
            qk, vt = _project(h, mod, None, g, w, gains, rope, ep, tm=tm)
            qk_c, vt_c = _project(hc, mod, ctx_row, g, w, gains, no_rope, ep_c, tm=tm_ctx)
            o = _pair_attention(qk, 0, [(qk, n_pairs, vt), (qk_c, n_pairs, vt_c)], n_pairs,
                                lambda p: p // rep_pairs, tq=tq)
            if not last:
                oc = _pair_attention(qk_c, 0, [(qk_c, n_pairs, vt_c)], n_pairs, lambda p: p // rep_pairs, tq=l_ctx)
            w_o = (gqa_wo_all, (j,))
        else:
            w = (diff_w_all, (j,))
            lam_init = 0.8 - 0.6 * math.exp(-0.3 * i)
            ep = [_Ep(rope=True, qscale=True)] * n_pairs + [_Ep(rope=True)] * n_pairs + [value_t] * n_pairs
            ep_c = [_Ep(qscale=True)] * n_pairs + [_Ep()] * n_pairs + [value_t] * n_pairs
            qk, vt = _project(h, mod, None, g, w, no_gain, rope, ep, tm=tm)
            qk_c, vt_c = _project(hc, mod, ctx_row, g, w, no_gain, no_rope, ep_c, tm=tm_ctx)
            sub_g = diff_subln_g[j][:, None]
            o = _diff_attention(qk, [(qk, vt), (qk_c, vt_c)], diff_lam[j], sub_g, lam_init, tq=tq)
            if not last:
                oc = _diff_attention(qk_c, [(qk_c, vt_c)], diff_lam[j], sub_g, lam_init, tq=l_ctx)
            w_o = (diff_wo_all, (j,))

        h = _ffn(h, mod, None, g, win[1], wout[1], which=1, tm=tm, attn=o, w_o=w_o)
        if not last:
            hc = _ffn(hc, mod, ctx_row, g, win[1], wout[1], which=1, tm=tm_ctx, attn=oc, w_o=w_o)
    return h
```

```python
import functools
import math
from typing import NamedTuple, Optional

import jax
import jax.numpy as jnp
import numpy as np
from jax import lax
from jax.experimental import pallas as pl
from jax.experimental.pallas import tpu as pltpu

HEAD_DIM = 64
LANES = 128
GRID_W = 64
NA_WIN_H = 8
NA_WIN_W = 16
ROPE_THETA = 10000.0
EPS = 1e-6
N_MOD = 9
MASK_VALUE = -1e30
MAX_LAGGED_EXPONENT = 64.0
LOG2_E = math.log2(math.e)
QUERY_SCALE = HEAD_DIM ** -0.5 * LOG2_E
SUM_ROWS = 16
VT_ROWS = LANES + SUM_ROWS
VMEM_LIMIT_BYTES = 56 * 1024 * 1024

FFN_CHUNK = 256
PROJ_CHUNK = 256
KV_CHUNK = 256
SCORE_LOOKAHEAD = 5
NA_ROWS_PER_STEP = 32
NA_GROUP_ROWS = 2


def _params(*semantics):
    return pltpu.CompilerParams(dimension_semantics=semantics, vmem_limit_bytes=VMEM_LIMIT_BYTES)


def _resident(param):
    arr, lead = param
    rest = arr.shape[len(lead):]
    index = tuple(lead) + (0,) * len(rest)
    return pl.BlockSpec((None,) * len(lead) + rest, lambda *_: index, pipeline_mode=pl.Buffered(1))


def _rms(x, g):
    return x * lax.rsqrt(jnp.mean(x * x, axis=-1, keepdims=True) + EPS) * g


def _pre(x, g, shift, scale):
    return _rms(x, g) * (1.0 + scale) + shift


def _mod_kernel(c_ref, w_ref, b_ref, o_ref):
    c = c_ref[...]
    sc = (c * jax.nn.sigmoid(c)).astype(jnp.bfloat16)
    w = w_ref[0].astype(jnp.bfloat16)
    o_ref[0] = jnp.dot(sc, w, preferred_element_type=jnp.float32) + b_ref[0]


def _modulation(cond, w_mod, b_mod):
    depth, d, n = w_mod.shape
    r = cond.shape[0]
    tn = n // N_MOD
    return pl.pallas_call(
        _mod_kernel,
        grid=(depth, n // tn),
        in_specs=[
            pl.BlockSpec((r, d), lambda i, j: (0, 0)),
            pl.BlockSpec((1, d, tn), lambda i, j: (i, 0, j)),
            pl.BlockSpec((1, 1, tn), lambda i, j: (i, 0, j)),
        ],
        out_specs=pl.BlockSpec((1, r, tn), lambda i, j: (i, 0, j)),
        out_shape=jax.ShapeDtypeStruct((depth, r, n), jnp.float32),
        compiler_params=_params("arbitrary", "arbitrary"),
        name="modulation",
    )(cond, w_mod, b_mod.reshape(depth, 1, n))


def _ffn_kernel(x_ref, mod_ref, g_ref, win_ref, wout_ref, *rest, mod_row, g_row):
    o_ref, act_ref = rest[-2:]
    d_ff = wout_ref.shape[0]
    x = x_ref[0]
    if len(rest) == 4:
        attn_ref, wo_ref = rest[:2]
        y = jnp.dot(attn_ref[0], wo_ref[...], preferred_element_type=jnp.float32)
        x = x + mod_ref[0, 5:6, :] * _rms(y, g_ref[3:4, :])
    shift = mod_ref[0, mod_row:mod_row + 1, :]
    scale = mod_ref[0, mod_row + 1:mod_row + 2, :]
    gate = mod_ref[0, mod_row + 2:mod_row + 3, :]
    xm = _pre(x, g_ref[g_row:g_row + 1, :], shift, scale).astype(jnp.bfloat16)
    for j in range(d_ff // FFN_CHUNK):
        lo = j * FFN_CHUNK
        a = jnp.dot(xm, win_ref[:, lo:lo + FFN_CHUNK], preferred_element_type=jnp.float32)
        b = jnp.dot(xm, win_ref[:, d_ff + lo:d_ff + lo + FFN_CHUNK], preferred_element_type=jnp.float32)
        act_ref[:, lo:lo + FFN_CHUNK] = (a * jax.nn.sigmoid(a) * b).astype(jnp.bfloat16)
    y = jnp.dot(act_ref[...], wout_ref[...], preferred_element_type=jnp.float32)
    o_ref[0] = x + 0.5 * gate * _rms(y, g_ref[g_row + 1:g_row + 2, :])


def _mod_spec(mod, cond_row):
    arr, (layer,) = mod
    block = (None, 1) + arr.shape[2:]
    if cond_row is None:
        return pl.BlockSpec(block, lambda b, i: (layer, b, 0, 0))
    return pl.BlockSpec(block, lambda b, i: (layer, cond_row, 0, 0))


def _ffn(h, mod, cond_row, g, win, wout, *, which, tm, attn=None, w_o=None):
    bsz, t, d = h.shape
    d_ff = wout[0].shape[-2]
    kern = functools.partial(_ffn_kernel, mod_row=6 * which, g_row=4 * which)
    in_specs = [
        pl.BlockSpec((1, tm, d), lambda b, i: (b, i, 0)),
        _mod_spec(mod, cond_row),
        _resident(g),
        _resident(win),
        _resident(wout),
    ]
    args = [h, mod[0], g[0], win[0], wout[0]]
    if attn is not None:
        in_specs += [pl.BlockSpec((1, tm, d), lambda b, i: (b, i, 0)), _resident(w_o)]
        args += [attn, w_o[0]]
    return pl.pallas_call(
        kern,
        grid=(bsz, t // tm),
        in_specs=in_specs,
        out_specs=pl.BlockSpec((1, tm, d), lambda b, i: (b, i, 0)),
        out_shape=jax.ShapeDtypeStruct(h.shape, h.dtype),
        scratch_shapes=[pltpu.VMEM((tm, d_ff), jnp.bfloat16)],
        compiler_params=_params("parallel", "parallel"),
        name=f"ffn{which}",
    )(*args)


class _Ep(NamedTuple):
    norm: Optional[int] = None
    rope: bool = False
    qscale: bool = False
    token_major: bool = True
    feature_major: bool = False


def _head_rms(y, gain):
    lane = lax.broadcasted_iota(jnp.int32, y.shape, 1)
    low = lane < HEAD_DIM
    y2 = y * y
    ss_low = jnp.sum(jnp.where(low, y2, 0.0), axis=-1, keepdims=True)
    ss_high = jnp.sum(jnp.where(low, 0.0, y2), axis=-1, keepdims=True)
    r = jnp.where(low, lax.rsqrt(ss_low / HEAD_DIM + EPS), lax.rsqrt(ss_high / HEAD_DIM + EPS))
    return y * r * gain


def _rotary(y, cos, sin_up, sin_down):
    up = pltpu.roll(y, LANES - HEAD_DIM // 4, axis=1)
    down = pltpu.roll(y, HEAD_DIM // 4, axis=1)
    return y * cos + up * sin_up + down * sin_down


def _proj_kernel(x_ref, mod_ref, g_ref, w_ref, gain_ref, cos_ref, sup_ref, sdn_ref, *out_refs, epilogues):
    o_ref = out_refs[0]
    x = x_ref[0]
    shift = mod_ref[0, 3:4, :]
    scale = mod_ref[0, 4:5, :]
    xm = _pre(x, g_ref[2:3, :], shift, scale).astype(jnp.bfloat16)
    n_blocks = len(epilogues)
    per_dot = PROJ_CHUNK // LANES
    main_at = 0
    vt_row = 0
    for c in range(0, n_blocks, per_dot):
        width = min(per_dot, n_blocks - c) * LANES
        y_all = jnp.dot(xm, w_ref[:, c * LANES:c * LANES + width], preferred_element_type=jnp.float32)
        for s in range(width // LANES):
            ep = epilogues[c + s]
            y = y_all[:, s * LANES:(s + 1) * LANES]
            if ep.norm is not None:
                y = _head_rms(y, gain_ref[ep.norm:ep.norm + 1, :])
            if ep.rope:
                y = _rotary(y, cos_ref[...], sup_ref[...], sdn_ref[...])
            if ep.qscale:
                y = y * QUERY_SCALE
            if ep.token_major:
                o_ref[0, :, main_at * LANES:(main_at + 1) * LANES] = y.astype(o_ref.dtype)
                main_at += 1
            if ep.feature_major:
                vt_ref = out_refs[1]
                vt_ref[0, vt_row:vt_row + LANES, :] = y.T.astype(vt_ref.dtype)
                vt_ref[0, vt_row + LANES:vt_row + VT_ROWS, :] = jnp.ones((SUM_ROWS, y.shape[0]), vt_ref.dtype)
                vt_row += VT_ROWS


def _project(h, mod, cond_row, g, w, gains, rope, epilogues, *, tm):
    bsz, t, d = h.shape
    n_main = sum(ep.token_major for ep in epilogues)
    vt_rows = VT_ROWS * sum(ep.feature_major for ep in epilogues)
    cos, sin_up, sin_down = rope
    rope_spec = pl.BlockSpec((tm, LANES), lambda b, i: (i, 0))
    out_specs = [pl.BlockSpec((1, tm, n_main * LANES), lambda b, i: (b, i, 0))]
    out_shape = [jax.ShapeDtypeStruct((bsz, t, n_main * LANES), jnp.bfloat16)]
    if vt_rows:
        out_specs.append(pl.BlockSpec((1, vt_rows, tm), lambda b, i: (b, 0, i)))
        out_shape.append(jax.ShapeDtypeStruct((bsz, vt_rows, t), jnp.bfloat16))
    outs = pl.pallas_call(
        functools.partial(_proj_kernel, epilogues=tuple(epilogues)),
        grid=(bsz, t // tm),
        in_specs=[
            pl.BlockSpec((1, tm, d), lambda b, i: (b, i, 0)),
            _mod_spec(mod, cond_row),
            _resident(g),
            _resident(w),
            _resident((gains, ())),
            rope_spec, rope_spec, rope_spec,
        ],
        out_specs=out_specs,
        out_shape=out_shape,
        compiler_params=_params("parallel", "parallel"),
        name="qkv_proj",
    )(h, mod[0], g[0], w[0], gains, cos, sin_up, sin_down)
    return outs if vt_rows else (outs[0], None)


def _split_pair(q):
    lane = lax.broadcasted_iota(jnp.int32, q.shape, 1)
    zero = jnp.zeros_like(q)
    return jnp.concatenate([jnp.where(lane < HEAD_DIM, q, zero), jnp.where(lane < HEAD_DIM, zero, q)], axis=0)


def _split_pair_t(q):
    return _split_pair(q).T


def _merge_pair_t(o):
    tq = o.shape[1] // 2
    return jnp.concatenate([o[:HEAD_DIM, :tq], o[HEAD_DIM:, tq:]], axis=0).T


def _normalized(acc):
    return acc[:LANES] * (1.0 / acc[LANES:LANES + 1])


class _LaggedSoftmax:
    def __init__(self, n_queries):
        self.m = None
        self.ref = None
        self.jump = jnp.zeros((1, n_queries), jnp.float32)
        self.acc = jnp.zeros((VT_ROWS, n_queries), jnp.float32)

    def probs(self, s):
        chunk_max = jnp.max(s, axis=0, keepdims=True)
        if self.m is None:
            self.m = self.ref = chunk_max
        else:
            self.acc = self.acc * jnp.exp2(self.ref - self.m)
            self.ref = self.m
            self.jump = jnp.maximum(self.jump, chunk_max - self.ref)
            self.m = jnp.maximum(self.m, chunk_max)
        return jnp.exp2(s - self.ref).astype(jnp.bfloat16)

    def add(self, pv):
        self.acc = self.acc + pv

    def may_overflow(self):
        return jnp.max(self.jump) > MAX_LAGGED_EXPONENT


def _exact_attention(n_queries, sources, score, pv):
    m = jnp.full((1, n_queries), -jnp.inf, jnp.float32)
    acc = jnp.zeros((VT_ROWS, n_queries), jnp.float32)
    for length, payload in sources:
        assert length % KV_CHUNK == 0

        def body(j, carry, payload=payload):
            m, acc = carry
            start = pl.multiple_of(j * KV_CHUNK, KV_CHUNK)
            s = score(payload, start, KV_CHUNK)
            m_new = jnp.maximum(m, jnp.max(s, axis=0, keepdims=True))
            p = jnp.exp2(s - m_new).astype(jnp.bfloat16)
            return m_new, acc * jnp.exp2(m - m_new) + pv(payload, start, KV_CHUNK, p)

        m, acc = lax.fori_loop(0, length // KV_CHUNK, body, (m, acc))
    return _normalized(acc)


def _lagged_attention(n_queries, sources, score, pv):
    sm = _LaggedSoftmax(n_queries)
    work = [(payload, start, size) for length, payload in sources for start, size in _chunks(length)]
    _software_pipeline(len(work), lambda c: score(*work[c]), lambda c, s: sm.add(pv(*work[c], sm.probs(s))))
    return _normalized(sm.acc), sm.may_overflow()


def _chunks(length):
    return [(start, min(KV_CHUNK, length - start)) for start in range(0, length, KV_CHUNK)]


def _software_pipeline(n, produce, consume):
    pending = {}
    for c in range(n + SCORE_LOOKAHEAD):
        if c < n:
            pending[c] = produce(c)
        if c >= SCORE_LOOKAHEAD:
            consume(c - SCORE_LOOKAHEAD, pending.pop(c - SCORE_LOOKAHEAD))


def _pair_attn_kernel(*refs, n_src):
    q_ref = refs[0]
    kv_refs = refs[1:1 + 2 * n_src]
    o_ref = refs[1 + 2 * n_src]
    qt = _split_pair_t(q_ref[0])
    sources = [(kv_refs[2 * s].shape[1], (kv_refs[2 * s], kv_refs[2 * s + 1])) for s in range(n_src)]

    def score(src, start, size):
        return jnp.dot(src[0][0, pl.ds(start, size), :], qt, preferred_element_type=jnp.float32)

    def pv(src, start, size, p):
        return jnp.dot(src[1][0, :, pl.ds(start, size)], p, preferred_element_type=jnp.float32)

    o, may_overflow = _lagged_attention(qt.shape[1], sources, score, pv)
    o_ref[0] = _merge_pair_t(o).astype(o_ref.dtype)

    @pl.when(may_overflow)
    def _():
        o_ref[0] = _merge_pair_t(_exact_attention(qt.shape[1], sources, score, pv)).astype(o_ref.dtype)


def _pair_attention(q_arr, q_block0, sources, n_pairs, k_of_pair, *, tq):
    bsz, t, _ = q_arr.shape
    in_specs = [pl.BlockSpec((1, tq, LANES), lambda b, p, i: (b, i, q_block0 + p))]
    args = [q_arr]
    for k_arr, kb, vt_arr in sources:
        tk = k_arr.shape[1]
        in_specs.append(pl.BlockSpec((1, tk, LANES), lambda b, p, i, kb=kb: (b, 0, kb + k_of_pair(p))))
        in_specs.append(pl.BlockSpec((1, VT_ROWS, tk), lambda b, p, i: (b, k_of_pair(p), 0)))
        args += [k_arr, vt_arr]
    return pl.pallas_call(
        functools.partial(_pair_attn_kernel, n_src=len(sources)),
        grid=(bsz, n_pairs, t // tq),
        in_specs=in_specs,
        out_specs=pl.BlockSpec((1, tq, LANES), lambda b, p, i: (b, i, p)),
        out_shape=jax.ShapeDtypeStruct((bsz, t, n_pairs * LANES), jnp.bfloat16),
        compiler_params=_params("parallel", "parallel", "arbitrary"),
        name="pair_attention",
    )(*args)


def _diff_attn_kernel(*refs, n_src, lam_init):
    q1_ref, q2_ref, lam_ref, g_ref = refs[:4]
    kv_refs = refs[4:4 + 4 * n_src]
    o_ref = refs[4 + 4 * n_src]
    tq = q1_ref.shape[1]
    qts = (_split_pair_t(q1_ref[0]), _split_pair_t(q2_ref[0]))
    sources = [[(kv_refs[4 * s].shape[1], (kv_refs[4 * s + i], kv_refs[4 * s + 2], kv_refs[4 * s + 3]))
                for s in range(n_src)] for i in range(2)]

    def pv(src, start, size, p):
        return jnp.concatenate([
            jnp.dot(src[1][0, :, pl.ds(start, size)], p[:, :tq], preferred_element_type=jnp.float32),
            jnp.dot(src[2][0, :, pl.ds(start, size)], p[:, tq:], preferred_element_type=jnp.float32)], axis=1)

    def score_of(i):
        return lambda src, start, size: jnp.dot(src[0][0, pl.ds(start, size), :], qts[i],
                                                preferred_element_type=jnp.float32)

    lam = lam_ref[...]
    lam_full = (jnp.exp(jnp.sum(lam[0:1] * lam[1:2], axis=-1, keepdims=True))
                - jnp.exp(jnp.sum(lam[2:3] * lam[3:4], axis=-1, keepdims=True)) + lam_init)

    def finish(o1, o2):
        o = o1 - lam_full * o2
        o = o * lax.rsqrt(jnp.mean(o * o, axis=0, keepdims=True) + EPS) * g_ref[...] * (1.0 - lam_init)
        o_ref[0, :, :LANES] = o[:, :tq].T.astype(o_ref.dtype)
        o_ref[0, :, LANES:] = o[:, tq:].T.astype(o_ref.dtype)

    sms = (_LaggedSoftmax(2 * tq), _LaggedSoftmax(2 * tq))
    work = [(i, sources[i][s][1], start, size)
            for s in range(n_src) for start, size in _chunks(sources[0][s][0]) for i in range(2)]
    _software_pipeline(
        len(work),
        lambda c: score_of(work[c][0])(*work[c][1:]),
        lambda c, s: sms[work[c][0]].add(pv(*work[c][1:], sms[work[c][0]].probs(s))))
    finish(_normalized(sms[0].acc), _normalized(sms[1].acc))

    @pl.when(sms[0].may_overflow() | sms[1].may_overflow())
    def _():
        finish(*[_exact_attention(2 * tq, sources[i], score_of(i), pv) for i in range(2)])


def _diff_attention(q_arr, sources, lam, subln_g, lam_init, *, tq):
    bsz, t, _ = q_arr.shape
    n_pairs = 4
    in_specs = [
        pl.BlockSpec((1, tq, LANES), lambda b, p, i: (b, i, p)),
        pl.BlockSpec((1, tq, LANES), lambda b, p, i: (b, i, n_pairs + p)),
        pl.BlockSpec(lam.shape, lambda b, p, i: (0, 0)),
        pl.BlockSpec(subln_g.shape, lambda b, p, i: (0, 0)),
    ]
    args = [q_arr, q_arr, lam, subln_g]
    for k_arr, vt_arr in sources:
        tk = k_arr.shape[1]
        in_specs += [
            pl.BlockSpec((1, tk, LANES), lambda b, p, i: (b, 0, 2 * n_pairs + p)),
            pl.BlockSpec((1, tk, LANES), lambda b, p, i: (b, 0, 3 * n_pairs + p)),
            pl.BlockSpec((1, VT_ROWS, tk), lambda b, p, i: (b, 2 * p, 0)),
            pl.BlockSpec((1, VT_ROWS, tk), lambda b, p, i: (b, 2 * p + 1, 0)),
        ]
        args += [k_arr, k_arr, vt_arr, vt_arr]
    return pl.pallas_call(
        functools.partial(_diff_attn_kernel, n_src=len(sources), lam_init=lam_init),
        grid=(bsz, n_pairs, t // tq),
        in_specs=in_specs,
        out_specs=pl.BlockSpec((1, tq, 2 * LANES), lambda b, p, i: (b, i, p)),
        out_shape=jax.ShapeDtypeStruct((bsz, t, 2 * LANES * n_pairs), jnp.bfloat16),
        compiler_params=_params("parallel", "parallel", "arbitrary"),
        name="diff_attention",
    )(*args)


def _na_attn_kernel(q_ref, k_ref, v_ref, kc_ref, vct_ref, bias_ref, o_ref, *, n_rows):
    step = pl.program_id(2)
    tg = NA_GROUP_ROWS * GRID_W
    n_groups = NA_ROWS_PER_STEP // NA_GROUP_ROWS
    win_rows = NA_WIN_H + NA_GROUP_ROWS - 1
    half = NA_WIN_H // 2
    qts = [_split_pair_t(q_ref[0, g * tg:(g + 1) * tg, :]) for g in range(n_groups)]
    s_cx_all = jnp.dot(kc_ref[0], jnp.concatenate(qts, axis=1), preferred_element_type=jnp.float32)
    lane = lax.broadcasted_iota(jnp.int32, (1, 2 * tg), 1)
    first_row = (lane & GRID_W) == 0

    def window(g):
        r0 = step * NA_ROWS_PER_STEP + NA_GROUP_ROWS * g
        rs0 = jnp.clip(r0 - half, 0, n_rows - NA_WIN_H)
        rs1 = jnp.clip(r0 + 1 - half, 0, n_rows - NA_WIN_H)
        ws = jnp.minimum(rs0, n_rows - win_rows)
        return r0, rs0, rs1, ws, pl.multiple_of(ws * GRID_W, GRID_W)

    def scores(g):
        r0, rs0, rs1, ws, tok0 = window(g)
        pieces = []
        for kr in range(win_rows):
            key_row = ws + kr
            e = jnp.clip(key_row - r0 + NA_WIN_H - 1, 0, 2 * NA_WIN_H - 1)
            out0 = jnp.where((key_row >= rs0) & (key_row < rs0 + NA_WIN_H), 0.0, MASK_VALUE)
            out1 = jnp.where((key_row >= rs1) & (key_row < rs1 + NA_WIN_H), 0.0, MASK_VALUE)
            pieces.append(bias_ref[0, e] + jnp.where(first_row, out0, out1))
        k_nb = k_ref[0, pl.ds(tok0, win_rows * GRID_W), :]
        return jnp.dot(k_nb, qts[g], preferred_element_type=jnp.float32) + jnp.concatenate(pieces, axis=0)

    def consume(g, s_nb):
        tok0 = window(g)[-1]
        vt_nb = jnp.concatenate([v_ref[0, pl.ds(tok0, win_rows * GRID_W), :].T,
                                 jnp.ones((SUM_ROWS, win_rows * GRID_W), jnp.bfloat16)], axis=0)
        s_cx = s_cx_all[:, g * 2 * tg:(g + 1) * 2 * tg]
        m = jnp.maximum(jnp.max(s_nb, axis=0, keepdims=True), jnp.max(s_cx, axis=0, keepdims=True))
        p_nb = jnp.exp2(s_nb - m).astype(jnp.bfloat16)
        p_cx = jnp.exp2(s_cx - m).astype(jnp.bfloat16)
        o = (jnp.dot(vt_nb, p_nb, preferred_element_type=jnp.float32)
             + jnp.dot(vct_ref[0], p_cx, preferred_element_type=jnp.float32))
        o = o[:LANES] * (1.0 / o[LANES:LANES + 1])
        o_ref[0, g * tg:(g + 1) * tg, :] = _merge_pair_t(o).astype(o_ref.dtype)

    _software_pipeline(n_groups, scores, consume)


def _na_attention(qkv, qk_ctx, vt_ctx, bias):
    bsz, s, _ = qkv.shape
    l_ctx = qk_ctx.shape[1]
    n_pairs = bias.shape[0]
    tq = NA_ROWS_PER_STEP * GRID_W
    return pl.pallas_call(
        functools.partial(_na_attn_kernel, n_rows=s // GRID_W),
        grid=(bsz, n_pairs, s // tq),
        in_specs=[
            pl.BlockSpec((1, tq, LANES), lambda b, p, i: (b, i, p)),
            pl.BlockSpec((1, s, LANES), lambda b, p, i: (b, 0, n_pairs + p)),
            pl.BlockSpec((1, s, LANES), lambda b, p, i: (b, 0, 2 * n_pairs + p)),
            pl.BlockSpec((1, l_ctx, LANES), lambda b, p, i: (b, 0, n_pairs + p)),
            pl.BlockSpec((1, VT_ROWS, l_ctx), lambda b, p, i: (b, p, 0)),
            pl.BlockSpec((1,) + bias.shape[1:], lambda b, p, i: (p, 0, 0, 0)),
        ],
        out_specs=pl.BlockSpec((1, tq, LANES), lambda b, p, i: (b, i, p)),
        out_shape=jax.ShapeDtypeStruct((bsz, s, n_pairs * LANES), jnp.bfloat16),
        compiler_params=_params("parallel", "parallel", "arbitrary"),
        name="na_attention",
    )(qkv, qkv, qkv, qk_ctx, vt_ctx, bias)


def _rope_tables(n):
    t = jnp.arange(n, dtype=jnp.int32)
    row = (t // GRID_W).astype(jnp.float32)
    col = (t % GRID_W).astype(jnp.float32)
    nf = HEAD_DIM // 4
    inv = 1.0 / (ROPE_THETA ** (jnp.arange(nf, dtype=jnp.float32) / nf))
    ar = row[:, None] * inv
    ac = col[:, None] * inv
    zero = jnp.zeros_like(ar)
    cos = jnp.concatenate([jnp.cos(ar), jnp.cos(ar), jnp.cos(ac), jnp.cos(ac)], axis=-1)
    sin_up = jnp.concatenate([-jnp.sin(ar), zero, -jnp.sin(ac), zero], axis=-1)
    sin_down = jnp.concatenate([zero, jnp.sin(ar), zero, jnp.sin(ac)], axis=-1)
    return tuple(jnp.tile(x, (1, LANES // HEAD_DIM)) for x in (cos, sin_up, sin_down))


def _na_bias_table(rpb):
    n_heads, n_rel, _ = rpb.shape
    pad = GRID_W - NA_WIN_W
    w = jnp.pad(rpb, ((0, 0), (0, 0), (pad, pad)))
    t = jnp.stack([w[:, :, GRID_W - 1 - q:2 * GRID_W - 1 - q] for q in range(GRID_W)], axis=2)
    t = jnp.swapaxes(t, 2, 3)
    kcol, qcol = np.arange(GRID_W)[:, None], np.arange(GRID_W)[None, :]
    cstart = np.clip(qcol - NA_WIN_W // 2, 0, GRID_W - NA_WIN_W)
    col_ok = (kcol >= cstart) & (kcol < cstart + NA_WIN_W)
    t = jnp.where(col_ok, t * LOG2_E, MASK_VALUE)
    first = jnp.concatenate([t, t[:, -1:]], axis=1)
    second = jnp.concatenate([t[:, :1], t], axis=1)
    both = jnp.stack([first, second], axis=3)
    both = both.reshape(n_heads // 2, 2, n_rel + 1, GRID_W, 2 * GRID_W).transpose(0, 2, 3, 1, 4)
    return both.reshape(n_heads // 2, n_rel + 1, GRID_W, 4 * GRID_W)


def _dup_heads(w):
    d, n = w.shape
    return jnp.tile(w.reshape(d, n // HEAD_DIM, 1, HEAD_DIM), (1, 1, 2, 1)).reshape(d, 2 * n)


def _tile_gain(g):
    return jnp.tile(g, LANES // HEAD_DIM)


def kernel(x, c, ctx, c_ctx, w_mod, b_mod, norm_g, w_ffn_in, w_ffn_out, na_w_qkv, na_rpb, na_w_o, gqa_w_qkv, gqa_q_gain, gqa_k_gain, gqa_w_o, diff_w_qkv, diff_lam, diff_subln_g, diff_w_o):
    bsz, seq, d = x.shape
    l_ctx = ctx.shape[1]
    depth = w_mod.shape[0]
    bf16 = jnp.bfloat16
    tm = 1024
    tm_ctx = l_ctx
    tq = 512
    n_pairs = d // LANES

    n_cond = 8 * ((bsz + 1 + 7) // 8)
    cond = jnp.concatenate([c, c_ctx[None], jnp.zeros((n_cond - bsz - 1, d), c.dtype)], axis=0)
    mod_all = _modulation(cond, w_mod, b_mod).reshape(depth, n_cond, N_MOD, d)
    ctx_row = bsz

    rope = _rope_tables(seq)
    no_rope = tuple(jnp.zeros((l_ctx, LANES), jnp.float32) for _ in range(3))
    no_gain = jnp.ones((2, LANES), jnp.float32)
    value_t = _Ep(token_major=False, feature_major=True)

    w_in_all, w_out_all = w_ffn_in.astype(bf16), w_ffn_out.astype(bf16)
    na_w_all, na_wo_all = na_w_qkv.astype(bf16), na_w_o.astype(bf16)
    gqa_wo_all = gqa_w_o.astype(bf16)
    diff_w_all, diff_wo_all = diff_w_qkv.astype(bf16), diff_w_o.astype(bf16)

    h, hc = x, ctx
    for i in range(depth):
        kind, j, last = i % 3, i // 3, i == depth - 1
        mod, g = (mod_all, (i,)), (norm_g, (i,))
        win, wout = [(w_in_all, (i, k)) for k in range(2)], [(w_out_all, (i, k)) for k in range(2)]

        h = _ffn(h, mod, None, g, win[0], wout[0], which=0, tm=tm)
        hc = _ffn(hc, mod, ctx_row, g, win[0], wout[0], which=0, tm=tm_ctx)

        if kind == 0:
            w = (na_w_all, (j,))
            ep = [_Ep(qscale=True)] * n_pairs + [_Ep()] * (2 * n_pairs)
            ep_c = [_Ep(qscale=True)] * n_pairs + [_Ep()] * n_pairs + [value_t] * n_pairs
            qkv, _ = _project(h, mod, None, g, w, no_gain, rope, ep, tm=tm)
            qkv_c, vt_c = _project(hc, mod, ctx_row, g, w, no_gain, no_rope, ep_c, tm=tm_ctx)
            o = _na_attention(qkv, qkv_c, vt_c, _na_bias_table(na_rpb[j]))
            if not last:
                oc = _pair_attention(qkv_c, 0, [(qkv_c, n_pairs, vt_c)], n_pairs, lambda p: p, tq=l_ctx)
            w_o = (na_wo_all, (j,))
        elif kind == 1:
            wf = gqa_w_qkv[j]
            dkv = (wf.shape[1] - d) // 2
            w = (jnp.concatenate([wf[:, :d], _dup_heads(wf[:, d:d + dkv]), _dup_heads(wf[:, d + dkv:])],
                                 axis=1).astype(bf16), ())
            gains = jnp.stack([_tile_gain(gqa_q_gain[j]), _tile_gain(gqa_k_gain[j])])
            n_kv = dkv // HEAD_DIM
            rep_pairs = n_pairs // n_kv
            ep = [_Ep(norm=0, rope=True, qscale=True)] * n_pairs + [_Ep(norm=1, rope=True)] * n_kv + [value_t] * n_kv
            ep_c = [_Ep(norm=0, qscale=True)] * n_pairs + [_Ep(norm=1)] * n_kv + [value_t] * n_kv
            qk, vt = _project(h, mod, None, g, w, gains, rope, ep, tm=tm)
            qk_c, vt_c = _project(hc, mod, ctx_row, g, w, gains, no_rope, ep_c, tm=tm_ctx)
            o = _pair_attention(qk, 0, [(qk, n_pairs, vt), (qk_c, n_pairs, vt_c)], n_pairs,
                                lambda p: p // rep_pairs, tq=tq)
            if not last:
                oc = _pair_attention(qk_c, 0, [(qk_c, n_pairs, vt_c)], n_pairs, lambda p: p // rep_pairs, tq=l_ctx)
            w_o = (gqa_wo_all, (j,))
        else:
            w = (diff_w_all, (j,))
            lam_init = 0.8 - 0.6 * math.exp(-0.3 * i)
            ep = [_Ep(rope=True, qscale=True)] * n_pairs + [_Ep(rope=True)] * n_pairs + [value_t] * n_pairs
            ep_c = [_Ep(qscale=True)] * n_pairs + [_Ep()] * n_pairs + [value_t] * n_pairs
            qk, vt = _project(h, mod, None, g, w, no_gain, rope, ep, tm=tm)
            qk_c, vt_c = _project(hc, mod, ctx_row, g, w, no_gain, no_rope, ep_c, tm=tm_ctx)
            sub_g = diff_subln_g[j][:, None]
            o = _diff_attention(qk, [(qk, vt), (qk_c, vt_c)], diff_lam[j], sub_g, lam_init, tq=tq)
            if not last:
                oc = _diff_attention(qk_c, [(qk_c, vt_c)], diff_lam[j], sub_g, lam_init, tq=l_ctx)
            w_o = (diff_wo_all, (j,))

        h = _ffn(h, mod, None, g, win[1], wout[1], which=1, tm=tm, attn=o, w_o=w_o)
        if not last:
            hc = _ffn(hc, mod, ctx_row, g, win[1], wout[1], which=1, tm=tm_ctx, attn=oc, w_o=w_o)
    return h
```

```python
import functools
import math
from typing import NamedTuple, Optional

import jax
import jax.numpy as jnp
import numpy as np
from jax import lax
from jax.experimental import pallas as pl
from jax.experimental.pallas import tpu as pltpu

HEAD_DIM = 64
LANES = 128
GRID_W = 64
NA_WIN_H = 8
NA_WIN_W = 16
ROPE_THETA = 10000.0
EPS = 1e-6
N_MOD = 9
MASK_VALUE = -1e30
MAX_LAGGED_EXPONENT = 64.0
LOG2_E = math.log2(math.e)
QUERY_SCALE = HEAD_DIM ** -0.5 * LOG2_E
SUM_ROWS = 16
VT_ROWS = LANES + SUM_ROWS
VMEM_LIMIT_BYTES = 56 * 1024 * 1024

FFN_CHUNK = 256
PROJ_CHUNK = 256
KV_CHUNK = 256
SCORE_LOOKAHEAD = 5
NA_LOOKAHEAD = 5
NA_ROWS_PER_STEP = 32
NA_GROUP_ROWS = 2


def _params(*semantics):
    return pltpu.CompilerParams(dimension_semantics=semantics, vmem_limit_bytes=VMEM_LIMIT_BYTES)


def _resident(param):
    arr, lead = param
    rest = arr.shape[len(lead):]
    index = tuple(lead) + (0,) * len(rest)
    return pl.BlockSpec((None,) * len(lead) + rest, lambda *_: index, pipeline_mode=pl.Buffered(1))


def _rms(x, g):
    return x * lax.rsqrt(jnp.mean(x * x, axis=-1, keepdims=True) + EPS) * g


def _pre(x, g, shift, scale):
    return _rms(x, g) * (1.0 + scale) + shift


def _mod_kernel(c_ref, w_ref, b_ref, o_ref):
    c = c_ref[...]
    sc = (c * jax.nn.sigmoid(c)).astype(jnp.bfloat16)
    w = w_ref[0].astype(jnp.bfloat16)
    o_ref[0] = jnp.dot(sc, w, preferred_element_type=jnp.float32) + b_ref[0]


def _modulation(cond, w_mod, b_mod):
    depth, d, n = w_mod.shape
    r = cond.shape[0]
    tn = n // N_MOD
    return pl.pallas_call(
        _mod_kernel,
        grid=(depth, n // tn),
        in_specs=[
            pl.BlockSpec((r, d), lambda i, j: (0, 0)),
            pl.BlockSpec((1, d, tn), lambda i, j: (i, 0, j)),
            pl.BlockSpec((1, 1, tn), lambda i, j: (i, 0, j)),
        ],
        out_specs=pl.BlockSpec((1, r, tn), lambda i, j: (i, 0, j)),
        out_shape=jax.ShapeDtypeStruct((depth, r, n), jnp.float32),
        compiler_params=_params("arbitrary", "arbitrary"),
        name="modulation",
    )(cond, w_mod, b_mod.reshape(depth, 1, n))


def _ffn_kernel(x_ref, mod_ref, g_ref, win_ref, wout_ref, *rest, mod_row, g_row):
    o_ref, act_ref = rest[-2:]
    d_ff = wout_ref.shape[0]
    x = x_ref[0]
    if len(rest) == 4:
        attn_ref, wo_ref = rest[:2]
        y = jnp.dot(attn_ref[0], wo_ref[...], preferred_element_type=jnp.float32)
        x = x + mod_ref[0, 5:6, :] * _rms(y, g_ref[3:4, :])
    shift = mod_ref[0, mod_row:mod_row + 1, :]
    scale = mod_ref[0, mod_row + 1:mod_row + 2, :]
    gate = mod_ref[0, mod_row + 2:mod_row + 3, :]
    xm = _pre(x, g_ref[g_row:g_row + 1, :], shift, scale).astype(jnp.bfloat16)
    for j in range(d_ff // FFN_CHUNK):
        lo = j * FFN_CHUNK
        a = jnp.dot(xm, win_ref[:, lo:lo + FFN_CHUNK], preferred_element_type=jnp.float32)
        b = jnp.dot(xm, win_ref[:, d_ff + lo:d_ff + lo + FFN_CHUNK], preferred_element_type=jnp.float32)
        act_ref[:, lo:lo + FFN_CHUNK] = (a * jax.nn.sigmoid(a) * b).astype(jnp.bfloat16)
    y = jnp.dot(act_ref[...], wout_ref[...], preferred_element_type=jnp.float32)
    o_ref[0] = x + 0.5 * gate * _rms(y, g_ref[g_row + 1:g_row + 2, :])


def _mod_spec(mod, cond_row):
    arr, (layer,) = mod
    block = (None, 1) + arr.shape[2:]
    if cond_row is None:
        return pl.BlockSpec(block, lambda b, i: (layer, b, 0, 0))
    return pl.BlockSpec(block, lambda b, i: (layer, cond_row, 0, 0))


def _ffn(h, mod, cond_row, g, win, wout, *, which, tm, attn=None, w_o=None):
    bsz, t, d = h.shape
    d_ff = wout[0].shape[-2]
    kern = functools.partial(_ffn_kernel, mod_row=6 * which, g_row=4 * which)
    in_specs = [
        pl.BlockSpec((1, tm, d), lambda b, i: (b, i, 0)),
        _mod_spec(mod, cond_row),
        _resident(g),
        _resident(win),
        _resident(wout),
    ]
    args = [h, mod[0], g[0], win[0], wout[0]]
    if attn is not None:
        in_specs += [pl.BlockSpec((1, tm, d), lambda b, i: (b, i, 0)), _resident(w_o)]
        args += [attn, w_o[0]]
    return pl.pallas_call(
        kern,
        grid=(bsz, t // tm),
        in_specs=in_specs,
        out_specs=pl.BlockSpec((1, tm, d), lambda b, i: (b, i, 0)),
        out_shape=jax.ShapeDtypeStruct(h.shape, h.dtype),
        scratch_shapes=[pltpu.VMEM((tm, d_ff), jnp.bfloat16)],
        compiler_params=_params("parallel", "parallel"),
        name=f"ffn{which}",
    )(*args)


class _Ep(NamedTuple):
    norm: Optional[int] = None
    rope: bool = False
    qscale: bool = False
    token_major: bool = True
    feature_major: bool = False


def _head_rms(y, gain):
    lane = lax.broadcasted_iota(jnp.int32, y.shape, 1)
    low = lane < HEAD_DIM
    y2 = y * y
    ss_low = jnp.sum(jnp.where(low, y2, 0.0), axis=-1, keepdims=True)
    ss_high = jnp.sum(jnp.where(low, 0.0, y2), axis=-1, keepdims=True)
    r = jnp.where(low, lax.rsqrt(ss_low / HEAD_DIM + EPS), lax.rsqrt(ss_high / HEAD_DIM + EPS))
    return y * r * gain


def _rotary(y, cos, sin_up, sin_down):
    up = pltpu.roll(y, LANES - HEAD_DIM // 4, axis=1)
    down = pltpu.roll(y, HEAD_DIM // 4, axis=1)
    return y * cos + up * sin_up + down * sin_down


def _proj_kernel(x_ref, mod_ref, g_ref, w_ref, gain_ref, cos_ref, sup_ref, sdn_ref, *out_refs, epilogues):
    o_ref = out_refs[0]
    x = x_ref[0]
    shift = mod_ref[0, 3:4, :]
    scale = mod_ref[0, 4:5, :]
    xm = _pre(x, g_ref[2:3, :], shift, scale).astype(jnp.bfloat16)
    n_blocks = len(epilogues)
    per_dot = PROJ_CHUNK // LANES
    main_at = 0
    vt_row = 0
    for c in range(0, n_blocks, per_dot):
        width = min(per_dot, n_blocks - c) * LANES
        y_all = jnp.dot(xm, w_ref[:, c * LANES:c * LANES + width], preferred_element_type=jnp.float32)
        for s in range(width // LANES):
            ep = epilogues[c + s]
            y = y_all[:, s * LANES:(s + 1) * LANES]
            if ep.norm is not None:
                y = _head_rms(y, gain_ref[ep.norm:ep.norm + 1, :])
            if ep.rope:
                y = _rotary(y, cos_ref[...], sup_ref[...], sdn_ref[...])
            if ep.qscale:
                y = y * QUERY_SCALE
            if ep.token_major:
                o_ref[0, :, main_at * LANES:(main_at + 1) * LANES] = y.astype(o_ref.dtype)
                main_at += 1
            if ep.feature_major:
                vt_ref = out_refs[1]
                vt_ref[0, vt_row:vt_row + LANES, :] = y.T.astype(vt_ref.dtype)
                vt_ref[0, vt_row + LANES:vt_row + VT_ROWS, :] = jnp.ones((SUM_ROWS, y.shape[0]), vt_ref.dtype)
                vt_row += VT_ROWS


def _project(h, mod, cond_row, g, w, gains, rope, epilogues, *, tm):
    bsz, t, d = h.shape
    n_main = sum(ep.token_major for ep in epilogues)
    vt_rows = VT_ROWS * sum(ep.feature_major for ep in epilogues)
    cos, sin_up, sin_down = rope
    rope_spec = pl.BlockSpec((tm, LANES), lambda b, i: (i, 0))
    out_specs = [pl.BlockSpec((1, tm, n_main * LANES), lambda b, i: (b, i, 0))]
    out_shape = [jax.ShapeDtypeStruct((bsz, t, n_main * LANES), jnp.bfloat16)]
    if vt_rows:
        out_specs.append(pl.BlockSpec((1, vt_rows, tm), lambda b, i: (b, 0, i)))
        out_shape.append(jax.ShapeDtypeStruct((bsz, vt_rows, t), jnp.bfloat16))
    outs = pl.pallas_call(
        functools.partial(_proj_kernel, epilogues=tuple(epilogues)),
        grid=(bsz, t // tm),
        in_specs=[
            pl.BlockSpec((1, tm, d), lambda b, i: (b, i, 0)),
            _mod_spec(mod, cond_row),
            _resident(g),
            _resident(w),
            _resident((gains, ())),
            rope_spec, rope_spec, rope_spec,
        ],
        out_specs=out_specs,
        out_shape=out_shape,
        compiler_params=_params("parallel", "parallel"),
        name="qkv_proj",
    )(h, mod[0], g[0], w[0], gains, cos, sin_up, sin_down)
    return outs if vt_rows else (outs[0], None)


def _split_pair(q):
    lane = lax.broadcasted_iota(jnp.int32, q.shape, 1)
    zero = jnp.zeros_like(q)
    return jnp.concatenate([jnp.where(lane < HEAD_DIM, q, zero), jnp.where(lane < HEAD_DIM, zero, q)], axis=0)


def _split_pair_t(q):
    return _split_pair(q).T


def _merge_pair_t(o):
    tq = o.shape[1] // 2
    return jnp.concatenate([o[:HEAD_DIM, :tq], o[HEAD_DIM:, tq:]], axis=0).T


def _normalized(acc):
    return acc[:LANES] * (1.0 / acc[LANES:LANES + 1])


class _LaggedSoftmax:
    def __init__(self, n_queries):
        self.m = None
        self.ref = None
        self.jump = jnp.zeros((1, n_queries), jnp.float32)
        self.acc = jnp.zeros((VT_ROWS, n_queries), jnp.float32)

    def probs(self, s):
        chunk_max = jnp.max(s, axis=0, keepdims=True)
        if self.m is None:
            self.m = self.ref = chunk_max
        else:
            self.acc = self.acc * jnp.exp2(self.ref - self.m)
            self.ref = self.m
            self.jump = jnp.maximum(self.jump, chunk_max - self.ref)
            self.m = jnp.maximum(self.m, chunk_max)
        return jnp.exp2(s - self.ref).astype(jnp.bfloat16)

    def add(self, pv):
        self.acc = self.acc + pv

    def may_overflow(self):
        return jnp.max(self.jump) > MAX_LAGGED_EXPONENT


def _exact_attention(n_queries, sources, score, pv):
    m = jnp.full((1, n_queries), -jnp.inf, jnp.float32)
    acc = jnp.zeros((VT_ROWS, n_queries), jnp.float32)
    for length, payload in sources:
        chunk = min(KV_CHUNK, length)
        assert length % chunk == 0

        def body(j, carry, payload=payload, chunk=chunk):
            m, acc = carry
            start = pl.multiple_of(j * chunk, chunk)
            s = score(payload, start, chunk)
            m_new = jnp.maximum(m, jnp.max(s, axis=0, keepdims=True))
            p = jnp.exp2(s - m_new).astype(jnp.bfloat16)
            return m_new, acc * jnp.exp2(m - m_new) + pv(payload, start, chunk, p)

        m, acc = lax.fori_loop(0, length // chunk, body, (m, acc))
    return _normalized(acc)


def _lagged_attention(n_queries, sources, score, pv):
    sm = _LaggedSoftmax(n_queries)
    work = [(payload, start, size) for length, payload in sources for start, size in _chunks(length)]
    _software_pipeline(len(work), lambda c: score(*work[c]), lambda c, s: sm.add(pv(*work[c], sm.probs(s))),
                       SCORE_LOOKAHEAD)
    return _normalized(sm.acc), sm.may_overflow()


def _chunks(length):
    return [(start, min(KV_CHUNK, length - start)) for start in range(0, length, KV_CHUNK)]


def _software_pipeline(n, produce, consume, lookahead):
    pending = {}
    for c in range(n + lookahead):
        if c < n:
            pending[c] = produce(c)
        if c >= lookahead:
            consume(c - lookahead, pending.pop(c - lookahead))


def _pair_attn_kernel(*refs, n_src):
    q_ref = refs[0]
    kv_refs = refs[1:1 + 2 * n_src]
    o_ref = refs[1 + 2 * n_src]
    qt = _split_pair_t(q_ref[0])
    sources = [(kv_refs[2 * s].shape[1], (kv_refs[2 * s], kv_refs[2 * s + 1])) for s in range(n_src)]

    def score(src, start, size):
        return jnp.dot(src[0][0, pl.ds(start, size), :], qt, preferred_element_type=jnp.float32)

    def pv(src, start, size, p):
        return jnp.dot(src[1][0, :, pl.ds(start, size)], p, preferred_element_type=jnp.float32)

    o, may_overflow = _lagged_attention(qt.shape[1], sources, score, pv)
    o_ref[0] = _merge_pair_t(o).astype(o_ref.dtype)

    @pl.when(may_overflow)
    def _():
        o_ref[0] = _merge_pair_t(_exact_attention(qt.shape[1], sources, score, pv)).astype(o_ref.dtype)


def _pair_attention(q_arr, q_block0, sources, n_pairs, k_of_pair, *, tq):
    bsz, t, _ = q_arr.shape
    in_specs = [pl.BlockSpec((1, tq, LANES), lambda b, p, i: (b, i, q_block0 + p))]
    args = [q_arr]
    for k_arr, kb, vt_arr in sources:
        tk = k_arr.shape[1]
        in_specs.append(pl.BlockSpec((1, tk, LANES), lambda b, p, i, kb=kb: (b, 0, kb + k_of_pair(p))))
        in_specs.append(pl.BlockSpec((1, VT_ROWS, tk), lambda b, p, i: (b, k_of_pair(p), 0)))
        args += [k_arr, vt_arr]
    return pl.pallas_call(
        functools.partial(_pair_attn_kernel, n_src=len(sources)),
        grid=(bsz, n_pairs, t // tq),
        in_specs=in_specs,
        out_specs=pl.BlockSpec((1, tq, LANES), lambda b, p, i: (b, i, p)),
        out_shape=jax.ShapeDtypeStruct((bsz, t, n_pairs * LANES), jnp.bfloat16),
        compiler_params=_params("parallel", "parallel", "arbitrary"),
        name="pair_attention",
    )(*args)


def _diff_attn_kernel(*refs, n_src, lam_init):
    q1_ref, q2_ref, lam_ref, g_ref = refs[:4]
    kv_refs = refs[4:4 + 4 * n_src]
    o_ref = refs[4 + 4 * n_src]
    tq = q1_ref.shape[1]
    qts = (_split_pair_t(q1_ref[0]), _split_pair_t(q2_ref[0]))
    sources = [[(kv_refs[4 * s].shape[1], (kv_refs[4 * s + i], kv_refs[4 * s + 2], kv_refs[4 * s + 3]))
                for s in range(n_src)] for i in range(2)]

    def pv(src, start, size, p):
        return jnp.concatenate([
            jnp.dot(src[1][0, :, pl.ds(start, size)], p[:, :tq], preferred_element_type=jnp.float32),
            jnp.dot(src[2][0, :, pl.ds(start, size)], p[:, tq:], preferred_element_type=jnp.float32)], axis=1)

    def score_of(i):
        return lambda src, start, size: jnp.dot(src[0][0, pl.ds(start, size), :], qts[i],
                                                preferred_element_type=jnp.float32)

    lam = lam_ref[...]
    lam_full = (jnp.exp(jnp.sum(lam[0:1] * lam[1:2], axis=-1, keepdims=True))
                - jnp.exp(jnp.sum(lam[2:3] * lam[3:4], axis=-1, keepdims=True)) + lam_init)

    def finish(o1, o2):
        o = o1 - lam_full * o2
        o = o * lax.rsqrt(jnp.mean(o * o, axis=0, keepdims=True) + EPS) * g_ref[...] * (1.0 - lam_init)
        o_ref[0, :, :LANES] = o[:, :tq].T.astype(o_ref.dtype)
        o_ref[0, :, LANES:] = o[:, tq:].T.astype(o_ref.dtype)

    sms = (_LaggedSoftmax(2 * tq), _LaggedSoftmax(2 * tq))
    work = [(i, sources[i][s][1], start, size)
            for s in range(n_src) for start, size in _chunks(sources[0][s][0]) for i in range(2)]
    _software_pipeline(
        len(work),
        lambda c: score_of(work[c][0])(*work[c][1:]),
        lambda c, s: sms[work[c][0]].add(pv(*work[c][1:], sms[work[c][0]].probs(s))),
        SCORE_LOOKAHEAD)
    finish(_normalized(sms[0].acc), _normalized(sms[1].acc))

    @pl.when(sms[0].may_overflow() | sms[1].may_overflow())
    def _():
        finish(*[_exact_attention(2 * tq, sources[i], score_of(i), pv) for i in range(2)])


def _diff_attention(q_arr, sources, lam, subln_g, lam_init, *, tq):
    bsz, t, _ = q_arr.shape
    n_pairs = 4
    in_specs = [
        pl.BlockSpec((1, tq, LANES), lambda b, p, i: (b, i, p)),
        pl.BlockSpec((1, tq, LANES), lambda b, p, i: (b, i, n_pairs + p)),
        pl.BlockSpec(lam.shape, lambda b, p, i: (0, 0)),
        pl.BlockSpec(subln_g.shape, lambda b, p, i: (0, 0)),
    ]
    args = [q_arr, q_arr, lam, subln_g]
    for k_arr, vt_arr in sources:
        tk = k_arr.shape[1]
        in_specs += [
            pl.BlockSpec((1, tk, LANES), lambda b, p, i: (b, 0, 2 * n_pairs + p)),
            pl.BlockSpec((1, tk, LANES), lambda b, p, i: (b, 0, 3 * n_pairs + p)),
            pl.BlockSpec((1, VT_ROWS, tk), lambda b, p, i: (b, 2 * p, 0)),
            pl.BlockSpec((1, VT_ROWS, tk), lambda b, p, i: (b, 2 * p + 1, 0)),
        ]
        args += [k_arr, k_arr, vt_arr, vt_arr]
    return pl.pallas_call(
        functools.partial(_diff_attn_kernel, n_src=len(sources), lam_init=lam_init),
        grid=(bsz, n_pairs, t // tq),
        in_specs=in_specs,
        out_specs=pl.BlockSpec((1, tq, 2 * LANES), lambda b, p, i: (b, i, p)),
        out_shape=jax.ShapeDtypeStruct((bsz, t, 2 * LANES * n_pairs), jnp.bfloat16),
        compiler_params=_params("parallel", "parallel", "arbitrary"),
        name="diff_attention",
    )(*args)


def _na_attn_kernel(q_ref, k_ref, v_ref, kc_ref, vct_ref, bias_ref, o_ref, *, n_rows):
    step = pl.program_id(2)
    tg = NA_GROUP_ROWS * GRID_W
    n_groups = NA_ROWS_PER_STEP // NA_GROUP_ROWS
    win_rows = NA_WIN_H + NA_GROUP_ROWS - 1
    half = NA_WIN_H // 2
    lane = lax.broadcasted_iota(jnp.int32, (1, 2 * tg), 1)
    first_row = (lane & GRID_W) == 0

    def window(g):
        r0 = step * NA_ROWS_PER_STEP + NA_GROUP_ROWS * g
        rs0 = jnp.clip(r0 - half, 0, n_rows - NA_WIN_H)
        rs1 = jnp.clip(r0 + 1 - half, 0, n_rows - NA_WIN_H)
        ws = jnp.minimum(rs0, n_rows - win_rows)
        return r0, rs0, rs1, ws, pl.multiple_of(ws * GRID_W, GRID_W)

    def queries_t(g):
        return _split_pair_t(q_ref[0, pl.ds(pl.multiple_of(g * tg, tg), tg), :])

    def window_scores(g, qt):
        r0, rs0, rs1, ws, tok0 = window(g)
        pieces = []
        for kr in range(win_rows):
            key_row = ws + kr
            e = jnp.clip(key_row - r0 + NA_WIN_H - 1, 0, 2 * NA_WIN_H - 1)
            out0 = jnp.where((key_row >= rs0) & (key_row < rs0 + NA_WIN_H), 0.0, MASK_VALUE)
            out1 = jnp.where((key_row >= rs1) & (key_row < rs1 + NA_WIN_H), 0.0, MASK_VALUE)
            pieces.append(bias_ref[0, e] + jnp.where(first_row, out0, out1))
        k_nb = k_ref[0, pl.ds(tok0, win_rows * GRID_W), :]
        return jnp.dot(k_nb, qt, preferred_element_type=jnp.float32) + jnp.concatenate(pieces, axis=0)

    def attend(g, s_nb, s_cx, ref):
        tok0 = window(g)[-1]
        vt_nb = jnp.concatenate([v_ref[0, pl.ds(tok0, win_rows * GRID_W), :].T,
                                 jnp.ones((SUM_ROWS, win_rows * GRID_W), jnp.bfloat16)], axis=0)
        p_nb = jnp.exp2(s_nb - ref).astype(jnp.bfloat16)
        p_cx = jnp.exp2(s_cx - ref).astype(jnp.bfloat16)
        acc = (jnp.dot(vt_nb, p_nb, preferred_element_type=jnp.float32)
               + jnp.dot(vct_ref[0], p_cx, preferred_element_type=jnp.float32))
        o_ref[0, pl.ds(pl.multiple_of(g * tg, tg), tg), :] = _merge_pair_t(_normalized(acc)).astype(o_ref.dtype)

    qts = [queries_t(g) for g in range(n_groups)]
    s_cx_all = jnp.dot(kc_ref[0], jnp.concatenate(qts, axis=1), preferred_element_type=jnp.float32)
    jump = [jnp.zeros((1, 2 * tg), jnp.float32)]

    def consume(g, s_nb):
        s_cx = s_cx_all[:, g * 2 * tg:(g + 1) * 2 * tg]
        ref = jnp.max(s_cx, axis=0, keepdims=True)
        jump[0] = jnp.maximum(jump[0], jnp.max(s_nb, axis=0, keepdims=True) - ref)
        attend(g, s_nb, s_cx, ref)

    _software_pipeline(n_groups, lambda g: window_scores(g, qts[g]), consume, NA_LOOKAHEAD)

    @pl.when(jnp.max(jump[0]) > MAX_LAGGED_EXPONENT)
    def _():
        def exact_group(g, carry):
            qt = queries_t(g)
            s_nb = window_scores(g, qt)
            s_cx = jnp.dot(kc_ref[0], qt, preferred_element_type=jnp.float32)
            attend(g, s_nb, s_cx, jnp.maximum(jnp.max(s_nb, axis=0, keepdims=True),
                                              jnp.max(s_cx, axis=0, keepdims=True)))
            return carry

        lax.fori_loop(0, n_groups, exact_group, 0)


def _na_attention(qkv, qk_ctx, vt_ctx, bias):
    bsz, s, _ = qkv.shape
    l_ctx = qk_ctx.shape[1]
    n_pairs = bias.shape[0]
    tq = NA_ROWS_PER_STEP * GRID_W
    return pl.pallas_call(
        functools.partial(_na_attn_kernel, n_rows=s // GRID_W),
        grid=(bsz, n_pairs, s // tq),
        in_specs=[
            pl.BlockSpec((1, tq, LANES), lambda b, p, i: (b, i, p)),
            pl.BlockSpec((1, s, LANES), lambda b, p, i: (b, 0, n_pairs + p)),
            pl.BlockSpec((1, s, LANES), lambda b, p, i: (b, 0, 2 * n_pairs + p)),
            pl.BlockSpec((1, l_ctx, LANES), lambda b, p, i: (b, 0, n_pairs + p)),
            pl.BlockSpec((1, VT_ROWS, l_ctx), lambda b, p, i: (b, p, 0)),
            pl.BlockSpec((1,) + bias.shape[1:], lambda b, p, i: (p, 0, 0, 0)),
        ],
        out_specs=pl.BlockSpec((1, tq, LANES), lambda b, p, i: (b, i, p)),
        out_shape=jax.ShapeDtypeStruct((bsz, s, n_pairs * LANES), jnp.bfloat16),
        compiler_params=_params("parallel", "parallel", "arbitrary"),
        name="na_attention",
    )(qkv, qkv, qkv, qk_ctx, vt_ctx, bias)


def _rope_tables(n):
    t = jnp.arange(n, dtype=jnp.int32)
    row = (t // GRID_W).astype(jnp.float32)
    col = (t % GRID_W).astype(jnp.float32)
    nf = HEAD_DIM // 4
    inv = 1.0 / (ROPE_THETA ** (jnp.arange(nf, dtype=jnp.float32) / nf))
    ar = row[:, None] * inv
    ac = col[:, None] * inv
    zero = jnp.zeros_like(ar)
    cos = jnp.concatenate([jnp.cos(ar), jnp.cos(ar), jnp.cos(ac), jnp.cos(ac)], axis=-1)
    sin_up = jnp.concatenate([-jnp.sin(ar), zero, -jnp.sin(ac), zero], axis=-1)
    sin_down = jnp.concatenate([zero, jnp.sin(ar), zero, jnp.sin(ac)], axis=-1)
    return tuple(jnp.tile(x, (1, LANES // HEAD_DIM)) for x in (cos, sin_up, sin_down))


def _na_bias_table(rpb):
    n_heads, n_rel, _ = rpb.shape
    pad = GRID_W - NA_WIN_W
    w = jnp.pad(rpb, ((0, 0), (0, 0), (pad, pad)))
    t = jnp.stack([w[:, :, GRID_W - 1 - q:2 * GRID_W - 1 - q] for q in range(GRID_W)], axis=2)
    t = jnp.swapaxes(t, 2, 3)
    kcol, qcol = np.arange(GRID_W)[:, None], np.arange(GRID_W)[None, :]
    cstart = np.clip(qcol - NA_WIN_W // 2, 0, GRID_W - NA_WIN_W)
    col_ok = (kcol >= cstart) & (kcol < cstart + NA_WIN_W)
    t = jnp.where(col_ok, t * LOG2_E, MASK_VALUE)
    first = jnp.concatenate([t, t[:, -1:]], axis=1)
    second = jnp.concatenate([t[:, :1], t], axis=1)
    both = jnp.stack([first, second], axis=3)
    both = both.reshape(n_heads // 2, 2, n_rel + 1, GRID_W, 2 * GRID_W).transpose(0, 2, 3, 1, 4)
    return both.reshape(n_heads // 2, n_rel + 1, GRID_W, 4 * GRID_W)


def _dup_heads(w):
    d, n = w.shape
    return jnp.tile(w.reshape(d, n // HEAD_DIM, 1, HEAD_DIM), (1, 1, 2, 1)).reshape(d, 2 * n)


def _tile_gain(g):
    return jnp.tile(g, LANES // HEAD_DIM)


def kernel(x, c, ctx, c_ctx, w_mod, b_mod, norm_g, w_ffn_in, w_ffn_out, na_w_qkv, na_rpb, na_w_o, gqa_w_qkv, gqa_q_gain, gqa_k_gain, gqa_w_o, diff_w_qkv, diff_lam, diff_subln_g, diff_w_o):
    bsz, seq, d = x.shape
    l_ctx = ctx.shape[1]
    depth = w_mod.shape[0]
    bf16 = jnp.bfloat16
    tm = 1024
    tm_ctx = l_ctx
    tq = 1024
    n_pairs = d // LANES

    n_cond = 8 * ((bsz + 1 + 7) // 8)
    cond = jnp.concatenate([c, c_ctx[None], jnp.zeros((n_cond - bsz - 1, d), c.dtype)], axis=0)
    mod_all = _modulation(cond, w_mod, b_mod).reshape(depth, n_cond, N_MOD, d)
    ctx_row = bsz

    rope = _rope_tables(seq)
    no_rope = tuple(jnp.zeros((l_ctx, LANES), jnp.float32) for _ in range(3))
    no_gain = jnp.ones((2, LANES), jnp.float32)
    value_t = _Ep(token_major=False, feature_major=True)

    w_in_all, w_out_all = w_ffn_in.astype(bf16), w_ffn_out.astype(bf16)
    na_w_all, na_wo_all = na_w_qkv.astype(bf16), na_w_o.astype(bf16)
    gqa_wo_all = gqa_w_o.astype(bf16)
    diff_w_all, diff_wo_all = diff_w_qkv.astype(bf16), diff_w_o.astype(bf16)

    h, hc = x, ctx
    for i in range(depth):
        kind, j, last = i % 3, i // 3, i == depth - 1
        mod, g = (mod_all, (i,)), (norm_g, (i,))
        win, wout = [(w_in_all, (i, k)) for k in range(2)], [(w_out_all, (i, k)) for k in range(2)]

        h = _ffn(h, mod, None, g, win[0], wout[0], which=0, tm=tm)
        hc = _ffn(hc, mod, ctx_row, g, win[0], wout[0], which=0, tm=tm_ctx)

        if kind == 0:
            w = (na_w_all, (j,))
            ep = [_Ep(qscale=True)] * n_pairs + [_Ep()] * (2 * n_pairs)
            ep_c = [_Ep(qscale=True)] * n_pairs + [_Ep()] * n_pairs + [value_t] * n_pairs
            qkv, _ = _project(h, mod, None, g, w, no_gain, rope, ep, tm=tm)
            qkv_c, vt_c = _project(hc, mod, ctx_row, g, w, no_gain, no_rope, ep_c, tm=tm_ctx)
            o = _na_attention(qkv, qkv_c, vt_c, _na_bias_table(na_rpb[j]))
            if not last:
                oc = _pair_attention(qkv_c, 0, [(qkv_c, n_pairs, vt_c)], n_pairs, lambda p: p, tq=l_ctx)
            w_o = (na_wo_all, (j,))
        elif kind == 1:
            wf = gqa_w_qkv[j]
            dkv = (wf.shape[1] - d) // 2
            w = (jnp.concatenate([wf[:, :d], _dup_heads(wf[:, d:d + dkv]), _dup_heads(wf[:, d + dkv:])],
                                 axis=1).astype(bf16), ())
            gains = jnp.stack([_tile_gain(gqa_q_gain[j]), _tile_gain(gqa_k_gain[j])])
            n_kv = dkv // HEAD_DIM
            rep_pairs = n_pairs // n_kv
            ep = [_Ep(norm=0, rope=True, qscale=True)] * n_pairs + [_Ep(norm=1, rope=True)] * n_kv + [value_t] * n_kv
            ep_c = [_Ep(norm=0, qscale=True)] * n_pairs + [_Ep(norm=1)] * n_kv + [value_t] * n_kv
            qk, vt = _project(h, mod, None, g, w, gains, rope, ep, tm=tm)
            qk_c, vt_c = _project(hc, mod, ctx_row, g, w, gains, no_rope, ep_c, tm=tm_ctx)
            o = _pair_attention(qk, 0, [(qk, n_pairs, vt), (qk_c, n_pairs, vt_c)], n_pairs,
                                lambda p: p // rep_pairs, tq=tq)
            if not last:
                oc = _pair_attention(qk_c, 0, [(qk_c, n_pairs, vt_c)], n_pairs, lambda p: p // rep_pairs, tq=l_ctx)
            w_o = (gqa_wo_all, (j,))
        else:
            w = (diff_w_all, (j,))
            lam_init = 0.8 - 0.6 * math.exp(-0.3 * i)
            ep = [_Ep(rope=True, qscale=True)] * n_pairs + [_Ep(rope=True)] * n_pairs + [value_t] * n_pairs
            ep_c = [_Ep(qscale=True)] * n_pairs + [_Ep()] * n_pairs + [value_t] * n_pairs
            qk, vt = _project(h, mod, None, g, w, no_gain, rope, ep, tm=tm)
            qk_c, vt_c = _project(hc, mod, ctx_row, g, w, no_gain, no_rope, ep_c, tm=tm_ctx)
            sub_g = diff_subln_g[j][:, None]
            o = _diff_attention(qk, [(qk, vt), (qk_c, vt_c)], diff_lam[j], sub_g, lam_init, tq=tq)
            if not last:
                oc = _diff_attention(qk_c, [(qk_c, vt_c)], diff_lam[j], sub_g, lam_init, tq=l_ctx)
            w_o = (diff_wo_all, (j,))

        h = _ffn(h, mod, None, g, win[1], wout[1], which=1, tm=tm, attn=o, w_o=w_o)
        if not last:
            hc = _ffn(hc, mod, ctx_row, g, win[1], wout[1], which=1, tm=tm_ctx, attn=oc, w_o=w_o)
    return h
```

```python
import functools
import math
from typing import NamedTuple, Optional

import jax
import jax.numpy as jnp
import numpy as np
from jax import lax
from jax.experimental import pallas as pl
from jax.experimental.pallas import tpu as pltpu

HEAD_DIM = 64
LANES = 128
GRID_W = 64
NA_WIN_H = 8
NA_WIN_W = 16
ROPE_THETA = 10000.0
EPS = 1e-6
N_MOD = 9
MASK_VALUE = -1e30
MAX_LAGGED_EXPONENT = 64.0
LOG2_E = math.log2(math.e)
QUERY_SCALE = HEAD_DIM ** -0.5 * LOG2_E
SUM_ROWS = 16
VT_ROWS = LANES + SUM_ROWS
VMEM_LIMIT_BYTES = 56 * 1024 * 1024

FFN_CHUNK = 256
PROJ_CHUNK = 256
KV_CHUNK = 256
SCORE_LOOKAHEAD = 5
NA_LOOKAHEAD = 5
NA_ROWS_PER_STEP = 64
NA_GROUP_ROWS = 2


def _params(*semantics):
    return pltpu.CompilerParams(dimension_semantics=semantics, vmem_limit_bytes=VMEM_LIMIT_BYTES)


def _resident(param):
    arr, lead = param
    rest = arr.shape[len(lead):]
    index = tuple(lead) + (0,) * len(rest)
    return pl.BlockSpec((None,) * len(lead) + rest, lambda *_: index, pipeline_mode=pl.Buffered(1))


def _rms(x, g):
    return x * lax.rsqrt(jnp.mean(x * x, axis=-1, keepdims=True) + EPS) * g


def _pre(x, g, shift, scale):
    return _rms(x, g) * (1.0 + scale) + shift


def _mod_kernel(c_ref, w_ref, b_ref, o_ref):
    c = c_ref[...]
    sc = (c * jax.nn.sigmoid(c)).astype(jnp.bfloat16)
    w = w_ref[0].astype(jnp.bfloat16)
    o_ref[0] = jnp.dot(sc, w, preferred_element_type=jnp.float32) + b_ref[0]


def _modulation(cond, w_mod, b_mod):
    depth, d, n = w_mod.shape
    r = cond.shape[0]
    tn = n // N_MOD
    return pl.pallas_call(
        _mod_kernel,
        grid=(depth, n // tn),
        in_specs=[
            pl.BlockSpec((r, d), lambda i, j: (0, 0)),
            pl.BlockSpec((1, d, tn), lambda i, j: (i, 0, j)),
            pl.BlockSpec((1, 1, tn), lambda i, j: (i, 0, j)),
        ],
        out_specs=pl.BlockSpec((1, r, tn), lambda i, j: (i, 0, j)),
        out_shape=jax.ShapeDtypeStruct((depth, r, n), jnp.float32),
        compiler_params=_params("arbitrary", "arbitrary"),
        name="modulation",
    )(cond, w_mod, b_mod.reshape(depth, 1, n))


def _ffn_kernel(x_ref, mod_ref, g_ref, win_ref, wout_ref, *rest, mod_row, g_row):
    o_ref, act_ref = rest[-2:]
    d_ff = wout_ref.shape[0]
    x = x_ref[0]
    if len(rest) == 4:
        attn_ref, wo_ref = rest[:2]
        y = jnp.dot(attn_ref[0], wo_ref[...], preferred_element_type=jnp.float32)
        x = x + mod_ref[0, 5:6, :] * _rms(y, g_ref[3:4, :])
    shift = mod_ref[0, mod_row:mod_row + 1, :]
    scale = mod_ref[0, mod_row + 1:mod_row + 2, :]
    gate = mod_ref[0, mod_row + 2:mod_row + 3, :]
    xm = _pre(x, g_ref[g_row:g_row + 1, :], shift, scale).astype(jnp.bfloat16)
    for j in range(d_ff // FFN_CHUNK):
        lo = j * FFN_CHUNK
        a = jnp.dot(xm, win_ref[:, lo:lo + FFN_CHUNK], preferred_element_type=jnp.float32)
        b = jnp.dot(xm, win_ref[:, d_ff + lo:d_ff + lo + FFN_CHUNK], preferred_element_type=jnp.float32)
        act_ref[:, lo:lo + FFN_CHUNK] = (a * jax.nn.sigmoid(a) * b).astype(jnp.bfloat16)
    y = jnp.dot(act_ref[...], wout_ref[...], preferred_element_type=jnp.float32)
    o_ref[0] = x + 0.5 * gate * _rms(y, g_ref[g_row + 1:g_row + 2, :])


def _mod_spec(mod, cond_row):
    arr, (layer,) = mod
    block = (None, 1) + arr.shape[2:]
    if cond_row is None:
        return pl.BlockSpec(block, lambda b, i: (layer, b, 0, 0))
    return pl.BlockSpec(block, lambda b, i: (layer, cond_row, 0, 0))


def _ffn(h, mod, cond_row, g, win, wout, *, which, tm, attn=None, w_o=None):
    bsz, t, d = h.shape
    d_ff = wout[0].shape[-2]
    kern = functools.partial(_ffn_kernel, mod_row=6 * which, g_row=4 * which)
    in_specs = [
        pl.BlockSpec((1, tm, d), lambda b, i: (b, i, 0)),
        _mod_spec(mod, cond_row),
        _resident(g),
        _resident(win),
        _resident(wout),
    ]
    args = [h, mod[0], g[0], win[0], wout[0]]
    if attn is not None:
        in_specs += [pl.BlockSpec((1, tm, d), lambda b, i: (b, i, 0)), _resident(w_o)]
        args += [attn, w_o[0]]
    return pl.pallas_call(
        kern,
        grid=(bsz, t // tm),
        in_specs=in_specs,
        out_specs=pl.BlockSpec((1, tm, d), lambda b, i: (b, i, 0)),
        out_shape=jax.ShapeDtypeStruct(h.shape, h.dtype),
        scratch_shapes=[pltpu.VMEM((tm, d_ff), jnp.bfloat16)],
        compiler_params=_params("parallel", "parallel"),
        name=f"ffn{which}",
    )(*args)


class _Ep(NamedTuple):
    norm: Optional[int] = None
    rope: bool = False
    qscale: bool = False
    token_major: bool = True
    feature_major: bool = False


def _head_rms(y, gain):
    lane = lax.broadcasted_iota(jnp.int32, y.shape, 1)
    low = lane < HEAD_DIM
    y2 = y * y
    ss_low = jnp.sum(jnp.where(low, y2, 0.0), axis=-1, keepdims=True)
    ss_high = jnp.sum(jnp.where(low, 0.0, y2), axis=-1, keepdims=True)
    r = jnp.where(low, lax.rsqrt(ss_low / HEAD_DIM + EPS), lax.rsqrt(ss_high / HEAD_DIM + EPS))
    return y * r * gain


def _rotary(y, cos, sin_up, sin_down):
    up = pltpu.roll(y, LANES - HEAD_DIM // 4, axis=1)
    down = pltpu.roll(y, HEAD_DIM // 4, axis=1)
    return y * cos + up * sin_up + down * sin_down


def _proj_kernel(x_ref, mod_ref, g_ref, w_ref, gain_ref, cos_ref, sup_ref, sdn_ref, *out_refs, epilogues):
    o_ref = out_refs[0]
    x = x_ref[0]
    shift = mod_ref[0, 3:4, :]
    scale = mod_ref[0, 4:5, :]
    xm = _pre(x, g_ref[2:3, :], shift, scale).astype(jnp.bfloat16)
    n_blocks = len(epilogues)
    per_dot = PROJ_CHUNK // LANES
    main_at = 0
    vt_row = 0
    for c in range(0, n_blocks, per_dot):
        width = min(per_dot, n_blocks - c) * LANES
        y_all = jnp.dot(xm, w_ref[:, c * LANES:c * LANES + width], preferred_element_type=jnp.float32)
        for s in range(width // LANES):
            ep = epilogues[c + s]
            y = y_all[:, s * LANES:(s + 1) * LANES]
            if ep.norm is not None:
                y = _head_rms(y, gain_ref[ep.norm:ep.norm + 1, :])
            if ep.rope:
                y = _rotary(y, cos_ref[...], sup_ref[...], sdn_ref[...])
            if ep.qscale:
                y = y * QUERY_SCALE
            if ep.token_major:
                o_ref[0, :, main_at * LANES:(main_at + 1) * LANES] = y.astype(o_ref.dtype)
                main_at += 1
            if ep.feature_major:
                vt_ref = out_refs[1]
                vt_ref[0, vt_row:vt_row + LANES, :] = y.T.astype(vt_ref.dtype)
                vt_ref[0, vt_row + LANES:vt_row + VT_ROWS, :] = jnp.ones((SUM_ROWS, y.shape[0]), vt_ref.dtype)
                vt_row += VT_ROWS


def _project(h, mod, cond_row, g, w, gains, rope, epilogues, *, tm):
    bsz, t, d = h.shape
    n_main = sum(ep.token_major for ep in epilogues)
    vt_rows = VT_ROWS * sum(ep.feature_major for ep in epilogues)
    cos, sin_up, sin_down = rope
    rope_spec = pl.BlockSpec((tm, LANES), lambda b, i: (i, 0))
    out_specs = [pl.BlockSpec((1, tm, n_main * LANES), lambda b, i: (b, i, 0))]
    out_shape = [jax.ShapeDtypeStruct((bsz, t, n_main * LANES), jnp.bfloat16)]
    if vt_rows:
        out_specs.append(pl.BlockSpec((1, vt_rows, tm), lambda b, i: (b, 0, i)))
        out_shape.append(jax.ShapeDtypeStruct((bsz, vt_rows, t), jnp.bfloat16))
    outs = pl.pallas_call(
        functools.partial(_proj_kernel, epilogues=tuple(epilogues)),
        grid=(bsz, t // tm),
        in_specs=[
            pl.BlockSpec((1, tm, d), lambda b, i: (b, i, 0)),
            _mod_spec(mod, cond_row),
            _resident(g),
            _resident(w),
            _resident((gains, ())),
            rope_spec, rope_spec, rope_spec,
        ],
        out_specs=out_specs,
        out_shape=out_shape,
        compiler_params=_params("parallel", "parallel"),
        name="qkv_proj",
    )(h, mod[0], g[0], w[0], gains, cos, sin_up, sin_down)
    return outs if vt_rows else (outs[0], None)


def _split_pair(q):
    lane = lax.broadcasted_iota(jnp.int32, q.shape, 1)
    zero = jnp.zeros_like(q)
    return jnp.concatenate([jnp.where(lane < HEAD_DIM, q, zero), jnp.where(lane < HEAD_DIM, zero, q)], axis=0)


def _split_pair_t(q):
    return _split_pair(q).T


def _merge_pair_t(o):
    tq = o.shape[1] // 2
    return jnp.concatenate([o[:HEAD_DIM, :tq], o[HEAD_DIM:, tq:]], axis=0).T


def _normalized(acc):
    return acc[:LANES] * (1.0 / acc[LANES:LANES + 1])


class _LaggedSoftmax:
    def __init__(self, n_queries):
        self.m = None
        self.ref = None
        self.jump = jnp.zeros((1, n_queries), jnp.float32)
        self.acc = jnp.zeros((VT_ROWS, n_queries), jnp.float32)

    def probs(self, s):
        chunk_max = jnp.max(s, axis=0, keepdims=True)
        if self.m is None:
            self.m = self.ref = chunk_max
        else:
            self.acc = self.acc * jnp.exp2(self.ref - self.m)
            self.ref = self.m
            self.jump = jnp.maximum(self.jump, chunk_max - self.ref)
            self.m = jnp.maximum(self.m, chunk_max)
        return jnp.exp2(s - self.ref).astype(jnp.bfloat16)

    def add(self, pv):
        self.acc = self.acc + pv

    def may_overflow(self):
        return jnp.max(self.jump) > MAX_LAGGED_EXPONENT


def _exact_attention(n_queries, sources, score, pv):
    m = jnp.full((1, n_queries), -jnp.inf, jnp.float32)
    acc = jnp.zeros((VT_ROWS, n_queries), jnp.float32)
    for length, payload in sources:
        chunk = min(KV_CHUNK, length)
        assert length % chunk == 0

        def body(j, carry, payload=payload, chunk=chunk):
            m, acc = carry
            start = pl.multiple_of(j * chunk, chunk)
            s = score(payload, start, chunk)
            m_new = jnp.maximum(m, jnp.max(s, axis=0, keepdims=True))
            p = jnp.exp2(s - m_new).astype(jnp.bfloat16)
            return m_new, acc * jnp.exp2(m - m_new) + pv(payload, start, chunk, p)

        m, acc = lax.fori_loop(0, length // chunk, body, (m, acc))
    return _normalized(acc)


def _lagged_attention(n_queries, sources, score, pv):
    sm = _LaggedSoftmax(n_queries)
    work = [(payload, start, size) for length, payload in sources for start, size in _chunks(length)]
    _software_pipeline(len(work), lambda c: score(*work[c]), lambda c, s: sm.add(pv(*work[c], sm.probs(s))),
                       SCORE_LOOKAHEAD)
    return _normalized(sm.acc), sm.may_overflow()


def _chunks(length):
    return [(start, min(KV_CHUNK, length - start)) for start in range(0, length, KV_CHUNK)]


def _software_pipeline(n, produce, consume, lookahead):
    pending = {}
    for c in range(n + lookahead):
        if c < n:
            pending[c] = produce(c)
        if c >= lookahead:
            consume(c - lookahead, pending.pop(c - lookahead))


def _pair_attn_kernel(*refs, n_src):
    q_ref = refs[0]
    kv_refs = refs[1:1 + 2 * n_src]
    o_ref = refs[1 + 2 * n_src]
    qt = _split_pair_t(q_ref[0])
    sources = [(kv_refs[2 * s].shape[1], (kv_refs[2 * s], kv_refs[2 * s + 1])) for s in range(n_src)]

    def score(src, start, size):
        return jnp.dot(src[0][0, pl.ds(start, size), :], qt, preferred_element_type=jnp.float32)

    def pv(src, start, size, p):
        return jnp.dot(src[1][0, :, pl.ds(start, size)], p, preferred_element_type=jnp.float32)

    o, may_overflow = _lagged_attention(qt.shape[1], sources, score, pv)
    o_ref[0] = _merge_pair_t(o).astype(o_ref.dtype)

    @pl.when(may_overflow)
    def _():
        o_ref[0] = _merge_pair_t(_exact_attention(qt.shape[1], sources, score, pv)).astype(o_ref.dtype)


def _pair_attention(q_arr, q_block0, sources, n_pairs, k_of_pair, *, tq):
    bsz, t, _ = q_arr.shape
    in_specs = [pl.BlockSpec((1, tq, LANES), lambda b, p, i: (b, i, q_block0 + p))]
    args = [q_arr]
    for k_arr, kb, vt_arr in sources:
        tk = k_arr.shape[1]
        in_specs.append(pl.BlockSpec((1, tk, LANES), lambda b, p, i, kb=kb: (b, 0, kb + k_of_pair(p))))
        in_specs.append(pl.BlockSpec((1, VT_ROWS, tk), lambda b, p, i: (b, k_of_pair(p), 0)))
        args += [k_arr, vt_arr]
    return pl.pallas_call(
        functools.partial(_pair_attn_kernel, n_src=len(sources)),
        grid=(bsz, n_pairs, t // tq),
        in_specs=in_specs,
        out_specs=pl.BlockSpec((1, tq, LANES), lambda b, p, i: (b, i, p)),
        out_shape=jax.ShapeDtypeStruct((bsz, t, n_pairs * LANES), jnp.bfloat16),
        compiler_params=_params("parallel", "parallel", "arbitrary"),
        name="pair_attention",
    )(*args)


def _diff_attn_kernel(*refs, n_src, lam_init):
    q1_ref, q2_ref, lam_ref, g_ref = refs[:4]
    kv_refs = refs[4:4 + 4 * n_src]
    o_ref = refs[4 + 4 * n_src]
    tq = q1_ref.shape[1]
    qts = (_split_pair_t(q1_ref[0]), _split_pair_t(q2_ref[0]))
    sources = [[(kv_refs[4 * s].shape[1], (kv_refs[4 * s + i], kv_refs[4 * s + 2], kv_refs[4 * s + 3]))
                for s in range(n_src)] for i in range(2)]

    def pv(src, start, size, p):
        return jnp.concatenate([
            jnp.dot(src[1][0, :, pl.ds(start, size)], p[:, :tq], preferred_element_type=jnp.float32),
            jnp.dot(src[2][0, :, pl.ds(start, size)], p[:, tq:], preferred_element_type=jnp.float32)], axis=1)

    def score_of(i):
        return lambda src, start, size: jnp.dot(src[0][0, pl.ds(start, size), :], qts[i],
                                                preferred_element_type=jnp.float32)

    lam = lam_ref[...]
    lam_full = (jnp.exp(jnp.sum(lam[0:1] * lam[1:2], axis=-1, keepdims=True))
                - jnp.exp(jnp.sum(lam[2:3] * lam[3:4], axis=-1, keepdims=True)) + lam_init)

    def finish(o1, o2):
        o = o1 - lam_full * o2
        o = o * lax.rsqrt(jnp.mean(o * o, axis=0, keepdims=True) + EPS) * g_ref[...] * (1.0 - lam_init)
        o_ref[0, :, :LANES] = o[:, :tq].T.astype(o_ref.dtype)
        o_ref[0, :, LANES:] = o[:, tq:].T.astype(o_ref.dtype)

    sms = (_LaggedSoftmax(2 * tq), _LaggedSoftmax(2 * tq))
    work = [(i, sources[i][s][1], start, size)
            for s in range(n_src) for start, size in _chunks(sources[0][s][0]) for i in range(2)]
    _software_pipeline(
        len(work),
        lambda c: score_of(work[c][0])(*work[c][1:]),
        lambda c, s: sms[work[c][0]].add(pv(*work[c][1:], sms[work[c][0]].probs(s))),
        SCORE_LOOKAHEAD)
    finish(_normalized(sms[0].acc), _normalized(sms[1].acc))

    @pl.when(sms[0].may_overflow() | sms[1].may_overflow())
    def _():
        finish(*[_exact_attention(2 * tq, sources[i], score_of(i), pv) for i in range(2)])


def _diff_attention(q_arr, sources, lam, subln_g, lam_init, *, tq):
    bsz, t, _ = q_arr.shape
    n_pairs = 4
    in_specs = [
        pl.BlockSpec((1, tq, LANES), lambda b, p, i: (b, i, p)),
        pl.BlockSpec((1, tq, LANES), lambda b, p, i: (b, i, n_pairs + p)),
        pl.BlockSpec(lam.shape, lambda b, p, i: (0, 0)),
        pl.BlockSpec(subln_g.shape, lambda b, p, i: (0, 0)),
    ]
    args = [q_arr, q_arr, lam, subln_g]
    for k_arr, vt_arr in sources:
        tk = k_arr.shape[1]
        in_specs += [
            pl.BlockSpec((1, tk, LANES), lambda b, p, i: (b, 0, 2 * n_pairs + p)),
            pl.BlockSpec((1, tk, LANES), lambda b, p, i: (b, 0, 3 * n_pairs + p)),
            pl.BlockSpec((1, VT_ROWS, tk), lambda b, p, i: (b, 2 * p, 0)),
            pl.BlockSpec((1, VT_ROWS, tk), lambda b, p, i: (b, 2 * p + 1, 0)),
        ]
        args += [k_arr, k_arr, vt_arr, vt_arr]
    return pl.pallas_call(
        functools.partial(_diff_attn_kernel, n_src=len(sources), lam_init=lam_init),
        grid=(bsz, n_pairs, t // tq),
        in_specs=in_specs,
        out_specs=pl.BlockSpec((1, tq, 2 * LANES), lambda b, p, i: (b, i, p)),
        out_shape=jax.ShapeDtypeStruct((bsz, t, 2 * LANES * n_pairs), jnp.bfloat16),
        compiler_params=_params("parallel", "parallel", "arbitrary"),
        name="diff_attention",
    )(*args)


def _na_attn_kernel(q_ref, k_ref, v_ref, kc_ref, vct_ref, bias_ref, o_ref, *, n_rows):
    step = pl.program_id(2)
    tg = NA_GROUP_ROWS * GRID_W
    n_groups = NA_ROWS_PER_STEP // NA_GROUP_ROWS
    win_rows = NA_WIN_H + NA_GROUP_ROWS - 1
    half = NA_WIN_H // 2
    lane = lax.broadcasted_iota(jnp.int32, (1, 2 * tg), 1)
    first_row = (lane & GRID_W) == 0

    def window(g):
        r0 = step * NA_ROWS_PER_STEP + NA_GROUP_ROWS * g
        rs0 = jnp.clip(r0 - half, 0, n_rows - NA_WIN_H)
        rs1 = jnp.clip(r0 + 1 - half, 0, n_rows - NA_WIN_H)
        ws = jnp.minimum(rs0, n_rows - win_rows)
        return r0, rs0, rs1, ws, pl.multiple_of(ws * GRID_W, GRID_W)

    def queries_t(g):
        return _split_pair_t(q_ref[0, pl.ds(pl.multiple_of(g * tg, tg), tg), :])

    def window_scores(g, qt):
        r0, rs0, rs1, ws, tok0 = window(g)
        pieces = []
        for kr in range(win_rows):
            key_row = ws + kr
            e = jnp.clip(key_row - r0 + NA_WIN_H - 1, 0, 2 * NA_WIN_H - 1)
            out0 = jnp.where((key_row >= rs0) & (key_row < rs0 + NA_WIN_H), 0.0, MASK_VALUE)
            out1 = jnp.where((key_row >= rs1) & (key_row < rs1 + NA_WIN_H), 0.0, MASK_VALUE)
            pieces.append(bias_ref[0, e] + jnp.where(first_row, out0, out1))
        k_nb = k_ref[0, pl.ds(tok0, win_rows * GRID_W), :]
        return jnp.dot(k_nb, qt, preferred_element_type=jnp.float32) + jnp.concatenate(pieces, axis=0)

    def attend(g, s_nb, s_cx, ref):
        tok0 = window(g)[-1]
        vt_nb = jnp.concatenate([v_ref[0, pl.ds(tok0, win_rows * GRID_W), :].T,
                                 jnp.ones((SUM_ROWS, win_rows * GRID_W), jnp.bfloat16)], axis=0)
        p_nb = jnp.exp2(s_nb - ref).astype(jnp.bfloat16)
        p_cx = jnp.exp2(s_cx - ref).astype(jnp.bfloat16)
        acc = (jnp.dot(vt_nb, p_nb, preferred_element_type=jnp.float32)
               + jnp.dot(vct_ref[0], p_cx, preferred_element_type=jnp.float32))
        o_ref[0, pl.ds(pl.multiple_of(g * tg, tg), tg), :] = _merge_pair_t(_normalized(acc)).astype(o_ref.dtype)

    qts = [queries_t(g) for g in range(n_groups)]
    s_cx_all = jnp.dot(kc_ref[0], jnp.concatenate(qts, axis=1), preferred_element_type=jnp.float32)
    jump = [jnp.zeros((1, 2 * tg), jnp.float32)]

    def consume(g, s_nb):
        s_cx = s_cx_all[:, g * 2 * tg:(g + 1) * 2 * tg]
        ref = jnp.max(s_cx, axis=0, keepdims=True)
        jump[0] = jnp.maximum(jump[0], jnp.max(s_nb, axis=0, keepdims=True) - ref)
        attend(g, s_nb, s_cx, ref)

    _software_pipeline(n_groups, lambda g: window_scores(g, qts[g]), consume, NA_LOOKAHEAD)

    @pl.when(jnp.max(jump[0]) > MAX_LAGGED_EXPONENT)
    def _():
        def exact_group(g, carry):
            qt = queries_t(g)
            s_nb = window_scores(g, qt)
            s_cx = jnp.dot(kc_ref[0], qt, preferred_element_type=jnp.float32)
            attend(g, s_nb, s_cx, jnp.maximum(jnp.max(s_nb, axis=0, keepdims=True),
                                              jnp.max(s_cx, axis=0, keepdims=True)))
            return carry

        lax.fori_loop(0, n_groups, exact_group, 0)


def _na_attention(qkv, qk_ctx, vt_ctx, bias):
    bsz, s, _ = qkv.shape
    l_ctx = qk_ctx.shape[1]
    n_pairs = bias.shape[0]
    tq = NA_ROWS_PER_STEP * GRID_W
    return pl.pallas_call(
        functools.partial(_na_attn_kernel, n_rows=s // GRID_W),
        grid=(bsz, n_pairs, s // tq),
        in_specs=[
            pl.BlockSpec((1, tq, LANES), lambda b, p, i: (b, i, p)),
            pl.BlockSpec((1, s, LANES), lambda b, p, i: (b, 0, n_pairs + p)),
            pl.BlockSpec((1, s, LANES), lambda b, p, i: (b, 0, 2 * n_pairs + p)),
            pl.BlockSpec((1, l_ctx, LANES), lambda b, p, i: (b, 0, n_pairs + p)),
            pl.BlockSpec((1, VT_ROWS, l_ctx), lambda b, p, i: (b, p, 0)),
            pl.BlockSpec((1,) + bias.shape[1:], lambda b, p, i: (p, 0, 0, 0)),
        ],
        out_specs=pl.BlockSpec((1, tq, LANES), lambda b, p, i: (b, i, p)),
        out_shape=jax.ShapeDtypeStruct((bsz, s, n_pairs * LANES), jnp.bfloat16),
        compiler_params=_params("parallel", "parallel", "arbitrary"),
        name="na_attention",
    )(qkv, qkv, qkv, qk_ctx, vt_ctx, bias)


def _rope_tables(n):
    t = jnp.arange(n, dtype=jnp.int32)
    row = (t // GRID_W).astype(jnp.float32)
    col = (t % GRID_W).astype(jnp.float32)
    nf = HEAD_DIM // 4
    inv = 1.0 / (ROPE_THETA ** (jnp.arange(nf, dtype=jnp.float32) / nf))
    ar = row[:, None] * inv
    ac = col[:, None] * inv
    zero = jnp.zeros_like(ar)
    cos = jnp.concatenate([jnp.cos(ar), jnp.cos(ar), jnp.cos(ac), jnp.cos(ac)], axis=-1)
    sin_up = jnp.concatenate([-jnp.sin(ar), zero, -jnp.sin(ac), zero], axis=-1)
    sin_down = jnp.concatenate([zero, jnp.sin(ar), zero, jnp.sin(ac)], axis=-1)
    return tuple(jnp.tile(x, (1, LANES // HEAD_DIM)) for x in (cos, sin_up, sin_down))


def _na_bias_table(rpb):
    n_heads, n_rel, _ = rpb.shape
    pad = GRID_W - NA_WIN_W
    w = jnp.pad(rpb, ((0, 0), (0, 0), (pad, pad)))
    t = jnp.stack([w[:, :, GRID_W - 1 - q:2 * GRID_W - 1 - q] for q in range(GRID_W)], axis=2)
    t = jnp.swapaxes(t, 2, 3)
    kcol, qcol = np.arange(GRID_W)[:, None], np.arange(GRID_W)[None, :]
    cstart = np.clip(qcol - NA_WIN_W // 2, 0, GRID_W - NA_WIN_W)
    col_ok = (kcol >= cstart) & (kcol < cstart + NA_WIN_W)
    t = jnp.where(col_ok, t * LOG2_E, MASK_VALUE)
    first = jnp.concatenate([t, t[:, -1:]], axis=1)
    second = jnp.concatenate([t[:, :1], t], axis=1)
    both = jnp.stack([first, second], axis=3)
    both = both.reshape(n_heads // 2, 2, n_rel + 1, GRID_W, 2 * GRID_W).transpose(0, 2, 3, 1, 4)
    return both.reshape(n_heads // 2, n_rel + 1, GRID_W, 4 * GRID_W)


def _dup_heads(w):
    d, n = w.shape
    return jnp.tile(w.reshape(d, n // HEAD_DIM, 1, HEAD_DIM), (1, 1, 2, 1)).reshape(d, 2 * n)


def _tile_gain(g):
    return jnp.tile(g, LANES // HEAD_DIM)


def kernel(x, c, ctx, c_ctx, w_mod, b_mod, norm_g, w_ffn_in, w_ffn_out, na_w_qkv, na_rpb, na_w_o, gqa_w_qkv, gqa_q_gain, gqa_k_gain, gqa_w_o, diff_w_qkv, diff_lam, diff_subln_g, diff_w_o):
    bsz, seq, d = x.shape
    l_ctx = ctx.shape[1]
    depth = w_mod.shape[0]
    bf16 = jnp.bfloat16
    tm = 1024
    tm_ctx = l_ctx
    tq = 1024
    n_pairs = d // LANES

    n_cond = 8 * ((bsz + 1 + 7) // 8)
    cond = jnp.concatenate([c, c_ctx[None], jnp.zeros((n_cond - bsz - 1, d), c.dtype)], axis=0)
    mod_all = _modulation(cond, w_mod, b_mod).reshape(depth, n_cond, N_MOD, d)
    ctx_row = bsz

    rope = _rope_tables(seq)
    no_rope = tuple(jnp.zeros((l_ctx, LANES), jnp.float32) for _ in range(3))
    no_gain = jnp.ones((2, LANES), jnp.float32)
    value_t = _Ep(token_major=False, feature_major=True)

    w_in_all, w_out_all = w_ffn_in.astype(bf16), w_ffn_out.astype(bf16)
    na_w_all, na_wo_all = na_w_qkv.astype(bf16), na_w_o.astype(bf16)
    gqa_wo_all = gqa_w_o.astype(bf16)
    diff_w_all, diff_wo_all = diff_w_qkv.astype(bf16), diff_w_o.astype(bf16)

    h, hc = x, ctx
    for i in range(depth):
        kind, j, last = i % 3, i // 3, i == depth - 1
        mod, g = (mod_all, (i,)), (norm_g, (i,))
        win, wout = [(w_in_all, (i, k)) for k in range(2)], [(w_out_all, (i, k)) for k in range(2)]

        h = _ffn(h, mod, None, g, win[0], wout[0], which=0, tm=tm)
        hc = _ffn(hc, mod, ctx_row, g, win[0], wout[0], which=0, tm=tm_ctx)

        if kind == 0:
            w = (na_w_all, (j,))
            ep = [_Ep(qscale=True)] * n_pairs + [_Ep()] * (2 * n_pairs)
            ep_c = [_Ep(qscale=True)] * n_pairs + [_Ep()] * n_pairs + [value_t] * n_pairs
            qkv, _ = _project(h, mod, None, g, w, no_gain, rope, ep, tm=tm)
            qkv_c, vt_c = _project(hc, mod, ctx_row, g, w, no_gain, no_rope, ep_c, tm=tm_ctx)
            o = _na_attention(qkv, qkv_c, vt_c, _na_bias_table(na_rpb[j]))
            if not last:
                oc = _pair_attention(qkv_c, 0, [(qkv_c, n_pairs, vt_c)], n_pairs, lambda p: p, tq=l_ctx)
            w_o = (na_wo_all, (j,))
        elif kind == 1:
            wf = gqa_w_qkv[j]
            dkv = (wf.shape[1] - d) // 2
            w = (jnp.concatenate([wf[:, :d], _dup_heads(wf[:, d:d + dkv]), _dup_heads(wf[:, d + dkv:])],
                                 axis=1).astype(bf16), ())
            gains = jnp.stack([_tile_gain(gqa_q_gain[j]), _tile_gain(gqa_k_gain[j])])
            n_kv = dkv // HEAD_DIM
            rep_pairs = n_pairs // n_kv
            ep = [_Ep(norm=0, rope=True, qscale=True)] * n_pairs + [_Ep(norm=1, rope=True)] * n_kv + [value_t] * n_kv
            ep_c = [_Ep(norm=0, qscale=True)] * n_pairs + [_Ep(norm=1)] * n_kv + [value_t] * n_kv
            qk, vt = _project(h, mod, None, g, w, gains, rope, ep, tm=tm)
            qk_c, vt_c = _project(hc, mod, ctx_row, g, w, gains, no_rope, ep_c, tm=tm_ctx)
            o = _pair_attention(qk, 0, [(qk, n_pairs, vt), (qk_c, n_pairs, vt_c)], n_pairs,
                                lambda p: p // rep_pairs, tq=tq)
            if not last:
                oc = _pair_attention(qk_c, 0, [(qk_c, n_pairs, vt_c)], n_pairs, lambda p: p // rep_pairs, tq=l_ctx)
            w_o = (gqa_wo_all, (j,))
        else:
            w = (diff_w_all, (j,))
            lam_init = 0.8 - 0.6 * math.exp(-0.3 * i)
            ep = [_Ep(rope=True, qscale=True)] * n_pairs + [_Ep(rope=True)] * n_pairs + [value_t] * n_pairs
            ep_c = [_Ep(qscale=True)] * n_pairs + [_Ep()] * n_pairs + [value_t] * n_pairs
            qk, vt = _project(h, mod, None, g, w, no_gain, rope, ep, tm=tm)
            qk_c, vt_c = _project(hc, mod, ctx_row, g, w, no_gain, no_rope, ep_c, tm=tm_ctx)
            sub_g = diff_subln_g[j][:, None]
            o = _diff_attention(qk, [(qk, vt), (qk_c, vt_c)], diff_lam[j], sub_g, lam_init, tq=tq)
            if not last:
                oc = _diff_attention(qk_c, [(qk_c, vt_c)], diff_lam[j], sub_g, lam_init, tq=l_ctx)
            w_o = (diff_wo_all, (j,))

        h = _ffn(h, mod, None, g, win[1], wout[1], which=1, tm=tm, attn=o, w_o=w_o)
        if not last:
            hc = _ffn(hc, mod, ctx_row, g, win[1], wout[1], which=1, tm=tm_ctx, attn=oc, w_o=w_o)
    return h
```

```python
import functools
import math
from typing import NamedTuple, Optional

import jax
import jax.numpy as jnp
import numpy as np
from jax import lax
from jax.experimental import pallas as pl
from jax.experimental.pallas import tpu as pltpu

HEAD_DIM = 64
LANES = 128
GRID_W = 64
NA_WIN_H = 8
NA_WIN_W = 16
ROPE_THETA = 10000.0
EPS = 1e-6
N_MOD = 9
MASK_VALUE = -1e30
MAX_LAGGED_EXPONENT = 64.0
LOG2_E = math.log2(math.e)
QUERY_SCALE = HEAD_DIM ** -0.5 * LOG2_E
SUM_ROWS = 16
VT_ROWS = LANES + SUM_ROWS
VMEM_LIMIT_BYTES = 56 * 1024 * 1024

FFN_CHUNK = 256
PROJ_CHUNK = 256
KV_CHUNK = 256
SCORE_LOOKAHEAD = 5
NA_LOOKAHEAD = 5
NA_ROWS_PER_STEP = 64
NA_GROUP_ROWS = 2


def _params(*semantics):
    return pltpu.CompilerParams(dimension_semantics=semantics, vmem_limit_bytes=VMEM_LIMIT_BYTES)


def _resident(param):
    arr, lead = param
    rest = arr.shape[len(lead):]
    index = tuple(lead) + (0,) * len(rest)
    return pl.BlockSpec((None,) * len(lead) + rest, lambda *_: index, pipeline_mode=pl.Buffered(1))


def _rms(x, g):
    return x * lax.rsqrt(jnp.mean(x * x, axis=-1, keepdims=True) + EPS) * g


def _pre(x, g, shift, scale):
    return _rms(x, g) * (1.0 + scale) + shift


def _mod_kernel(c_ref, w_ref, b_ref, o_ref):
    c = c_ref[...]
    sc = (c * jax.nn.sigmoid(c)).astype(jnp.bfloat16)
    w = w_ref[0].astype(jnp.bfloat16)
    o_ref[0] = jnp.dot(sc, w, preferred_element_type=jnp.float32) + b_ref[0]


def _modulation(cond, w_mod, b_mod):
    depth, d, n = w_mod.shape
    r = cond.shape[0]
    tn = n // N_MOD
    return pl.pallas_call(
        _mod_kernel,
        grid=(depth, n // tn),
        in_specs=[
            pl.BlockSpec((r, d), lambda i, j: (0, 0)),
            pl.BlockSpec((1, d, tn), lambda i, j: (i, 0, j)),
            pl.BlockSpec((1, 1, tn), lambda i, j: (i, 0, j)),
        ],
        out_specs=pl.BlockSpec((1, r, tn), lambda i, j: (i, 0, j)),
        out_shape=jax.ShapeDtypeStruct((depth, r, n), jnp.float32),
        compiler_params=_params("arbitrary", "arbitrary"),
        name="modulation",
    )(cond, w_mod, b_mod.reshape(depth, 1, n))


def _ffn_kernel(x_ref, mod_ref, g_ref, win_ref, wout_ref, *rest, mod_row, g_row):
    o_ref, act_ref = rest[-2:]
    d_ff = wout_ref.shape[0]
    x = x_ref[0]
    if len(rest) == 4:
        attn_ref, wo_ref = rest[:2]
        y = jnp.dot(attn_ref[0], wo_ref[...], preferred_element_type=jnp.float32)
        x = x + mod_ref[0, 5:6, :] * _rms(y, g_ref[3:4, :])
    shift = mod_ref[0, mod_row:mod_row + 1, :]
    scale = mod_ref[0, mod_row + 1:mod_row + 2, :]
    gate = mod_ref[0, mod_row + 2:mod_row + 3, :]
    xm = _pre(x, g_ref[g_row:g_row + 1, :], shift, scale).astype(jnp.bfloat16)
    for j in range(d_ff // FFN_CHUNK):
        lo = j * FFN_CHUNK
        a = jnp.dot(xm, win_ref[:, lo:lo + FFN_CHUNK], preferred_element_type=jnp.float32)
        b = jnp.dot(xm, win_ref[:, d_ff + lo:d_ff + lo + FFN_CHUNK], preferred_element_type=jnp.float32)
        act_ref[:, lo:lo + FFN_CHUNK] = (a * jax.nn.sigmoid(a) * b).astype(jnp.bfloat16)
    y = jnp.dot(act_ref[...], wout_ref[...], preferred_element_type=jnp.float32)
    o_ref[0] = x + 0.5 * gate * _rms(y, g_ref[g_row + 1:g_row + 2, :])


def _mod_spec(mod, cond_row):
    arr, (layer,) = mod
    block = (None, 1) + arr.shape[2:]
    if cond_row is None:
        return pl.BlockSpec(block, lambda b, i: (layer, b, 0, 0))
    return pl.BlockSpec(block, lambda b, i: (layer, cond_row, 0, 0))


def _ffn(h, mod, cond_row, g, win, wout, *, which, tm, attn=None, w_o=None):
    bsz, t, d = h.shape
    d_ff = wout[0].shape[-2]
    kern = functools.partial(_ffn_kernel, mod_row=6 * which, g_row=4 * which)
    in_specs = [
        pl.BlockSpec((1, tm, d), lambda b, i: (b, i, 0)),
        _mod_spec(mod, cond_row),
        _resident(g),
        _resident(win),
        _resident(wout),
    ]
    args = [h, mod[0], g[0], win[0], wout[0]]
    if attn is not None:
        in_specs += [pl.BlockSpec((1, tm, d), lambda b, i: (b, i, 0)), _resident(w_o)]
        args += [attn, w_o[0]]
    return pl.pallas_call(
        kern,
        grid=(bsz, t // tm),
        in_specs=in_specs,
        out_specs=pl.BlockSpec((1, tm, d), lambda b, i: (b, i, 0)),
        out_shape=jax.ShapeDtypeStruct(h.shape, h.dtype),
        scratch_shapes=[pltpu.VMEM((tm, d_ff), jnp.bfloat16)],
        compiler_params=_params("parallel", "parallel"),
        name=f"ffn{which}",
    )(*args)


class _Ep(NamedTuple):
    norm: Optional[int] = None
    rope: bool = False
    qscale: bool = False
    token_major: bool = True
    value_rows: int = 0


def _head_rms(y, gain):
    lane = lax.broadcasted_iota(jnp.int32, y.shape, 1)
    low = lane < HEAD_DIM
    y2 = y * y
    ss_low = jnp.sum(jnp.where(low, y2, 0.0), axis=-1, keepdims=True)
    ss_high = jnp.sum(jnp.where(low, 0.0, y2), axis=-1, keepdims=True)
    r = jnp.where(low, lax.rsqrt(ss_low / HEAD_DIM + EPS), lax.rsqrt(ss_high / HEAD_DIM + EPS))
    return y * r * gain


def _rotary(y, cos, sin_up, sin_down):
    up = pltpu.roll(y, LANES - HEAD_DIM // 4, axis=1)
    down = pltpu.roll(y, HEAD_DIM // 4, axis=1)
    return y * cos + up * sin_up + down * sin_down


def _proj_kernel(x_ref, mod_ref, g_ref, w_ref, gain_ref, cos_ref, sup_ref, sdn_ref, *out_refs, epilogues):
    o_ref = out_refs[0]
    x = x_ref[0]
    shift = mod_ref[0, 3:4, :]
    scale = mod_ref[0, 4:5, :]
    xm = _pre(x, g_ref[2:3, :], shift, scale).astype(jnp.bfloat16)
    n_blocks = len(epilogues)
    per_dot = PROJ_CHUNK // LANES
    main_at = 0
    vt_row = 0
    for c in range(0, n_blocks, per_dot):
        width = min(per_dot, n_blocks - c) * LANES
        y_all = jnp.dot(xm, w_ref[:, c * LANES:c * LANES + width], preferred_element_type=jnp.float32)
        for s in range(width // LANES):
            ep = epilogues[c + s]
            y = y_all[:, s * LANES:(s + 1) * LANES]
            if ep.norm is not None:
                y = _head_rms(y, gain_ref[ep.norm:ep.norm + 1, :])
            if ep.rope:
                y = _rotary(y, cos_ref[...], sup_ref[...], sdn_ref[...])
            if ep.qscale:
                y = y * QUERY_SCALE
            if ep.token_major:
                o_ref[0, :, main_at * LANES:(main_at + 1) * LANES] = y.astype(o_ref.dtype)
                main_at += 1
            if ep.value_rows:
                vt_ref = out_refs[1]
                yt = y.T.astype(vt_ref.dtype)
                for f0 in range(0, LANES, ep.value_rows):
                    vt_ref[0, vt_row:vt_row + ep.value_rows, :] = yt[f0:f0 + ep.value_rows]
                    vt_row += ep.value_rows
                    vt_ref[0, vt_row:vt_row + SUM_ROWS, :] = jnp.ones((SUM_ROWS, y.shape[0]), vt_ref.dtype)
                    vt_row += SUM_ROWS


def _project(h, mod, cond_row, g, w, gains, rope, epilogues, *, tm):
    bsz, t, d = h.shape
    n_main = sum(ep.token_major for ep in epilogues)
    vt_rows = sum(LANES // ep.value_rows * (ep.value_rows + SUM_ROWS) for ep in epilogues if ep.value_rows)
    cos, sin_up, sin_down = rope
    rope_spec = pl.BlockSpec((tm, LANES), lambda b, i: (i, 0))
    out_specs = [pl.BlockSpec((1, tm, n_main * LANES), lambda b, i: (b, i, 0))]
    out_shape = [jax.ShapeDtypeStruct((bsz, t, n_main * LANES), jnp.bfloat16)]
    if vt_rows:
        out_specs.append(pl.BlockSpec((1, vt_rows, tm), lambda b, i: (b, 0, i)))
        out_shape.append(jax.ShapeDtypeStruct((bsz, vt_rows, t), jnp.bfloat16))
    outs = pl.pallas_call(
        functools.partial(_proj_kernel, epilogues=tuple(epilogues)),
        grid=(bsz, t // tm),
        in_specs=[
            pl.BlockSpec((1, tm, d), lambda b, i: (b, i, 0)),
            _mod_spec(mod, cond_row),
            _resident(g),
            _resident(w),
            _resident((gains, ())),
            rope_spec, rope_spec, rope_spec,
        ],
        out_specs=out_specs,
        out_shape=out_shape,
        compiler_params=_params("parallel", "parallel"),
        name="qkv_proj",
    )(h, mod[0], g[0], w[0], gains, cos, sin_up, sin_down)
    return outs if vt_rows else (outs[0], None)


def _split_pair(q):
    lane = lax.broadcasted_iota(jnp.int32, q.shape, 1)
    zero = jnp.zeros_like(q)
    return jnp.concatenate([jnp.where(lane < HEAD_DIM, q, zero), jnp.where(lane < HEAD_DIM, zero, q)], axis=0)


def _split_pair_t(q):
    return _split_pair(q).T


def _merge_pair_t(o):
    tq = o.shape[1] // 2
    hi = o[HEAD_DIM:, tq:] if o.shape[0] == LANES else o[:, tq:]
    return jnp.concatenate([o[:HEAD_DIM, :tq], hi], axis=0).T


def _normalized(acc):
    features = acc.shape[0] - SUM_ROWS
    return acc[:features] * (1.0 / acc[features:features + 1])


class _LaggedSoftmax:
    def __init__(self, n_queries, acc_rows):
        self.m = None
        self.ref = None
        self.jump = jnp.zeros((1, n_queries), jnp.float32)
        self.acc = jnp.zeros((acc_rows, n_queries), jnp.float32)

    def probs(self, s):
        chunk_max = jnp.max(s, axis=0, keepdims=True)
        if self.m is None:
            self.m = self.ref = chunk_max
        else:
            self.acc = self.acc * jnp.exp2(self.ref - self.m)
            self.ref = self.m
            self.jump = jnp.maximum(self.jump, chunk_max - self.ref)
            self.m = jnp.maximum(self.m, chunk_max)
        return jnp.exp2(s - self.ref).astype(jnp.bfloat16)

    def add(self, pv):
        self.acc = self.acc + pv

    def may_overflow(self):
        return jnp.max(self.jump) > MAX_LAGGED_EXPONENT


def _exact_attention(n_queries, acc_rows, sources, score, pv):
    m = jnp.full((1, n_queries), -jnp.inf, jnp.float32)
    acc = jnp.zeros((acc_rows, n_queries), jnp.float32)
    for length, payload in sources:
        chunk = min(KV_CHUNK, length)
        assert length % chunk == 0

        def body(j, carry, payload=payload, chunk=chunk):
            m, acc = carry
            start = pl.multiple_of(j * chunk, chunk)
            s = score(payload, start, chunk)
            m_new = jnp.maximum(m, jnp.max(s, axis=0, keepdims=True))
            p = jnp.exp2(s - m_new).astype(jnp.bfloat16)
            return m_new, acc * jnp.exp2(m - m_new) + pv(payload, start, chunk, p)

        m, acc = lax.fori_loop(0, length // chunk, body, (m, acc))
    return _normalized(acc)


def _lagged_attention(n_queries, acc_rows, sources, score, pv):
    sm = _LaggedSoftmax(n_queries, acc_rows)
    work = [(payload, start, size) for length, payload in sources for start, size in _chunks(length)]
    _software_pipeline(len(work), lambda c: score(*work[c]), lambda c, s: sm.add(pv(*work[c], sm.probs(s))),
                       SCORE_LOOKAHEAD)
    return _normalized(sm.acc), sm.may_overflow()


def _chunks(length):
    return [(start, min(KV_CHUNK, length - start)) for start in range(0, length, KV_CHUNK)]


def _software_pipeline(n, produce, consume, lookahead):
    pending = {}
    for c in range(n + lookahead):
        if c < n:
            pending[c] = produce(c)
        if c >= lookahead:
            consume(c - lookahead, pending.pop(c - lookahead))


def _pair_attn_kernel(*refs, n_src):
    q_ref = refs[0]
    kv_refs = refs[1:1 + 2 * n_src]
    o_ref = refs[1 + 2 * n_src]
    qt = _split_pair_t(q_ref[0])
    sources = [(kv_refs[2 * s].shape[1], (kv_refs[2 * s], kv_refs[2 * s + 1])) for s in range(n_src)]

    def score(src, start, size):
        return jnp.dot(src[0][0, pl.ds(start, size), :], qt, preferred_element_type=jnp.float32)

    def pv(src, start, size, p):
        return jnp.dot(src[1][0, :, pl.ds(start, size)], p, preferred_element_type=jnp.float32)

    dims = (qt.shape[1], kv_refs[1].shape[1])
    o, may_overflow = _lagged_attention(*dims, sources, score, pv)
    o_ref[0] = _merge_pair_t(o).astype(o_ref.dtype)

    @pl.when(may_overflow)
    def _():
        o_ref[0] = _merge_pair_t(_exact_attention(*dims, sources, score, pv)).astype(o_ref.dtype)


def _pair_attention(q_arr, q_block0, sources, n_pairs, k_of_pair, *, tq, value_rows=LANES):
    bsz, t, _ = q_arr.shape
    in_specs = [pl.BlockSpec((1, tq, LANES), lambda b, p, i: (b, i, q_block0 + p))]
    args = [q_arr]
    for k_arr, kb, vt_arr in sources:
        tk = k_arr.shape[1]
        in_specs.append(pl.BlockSpec((1, tk, LANES), lambda b, p, i, kb=kb: (b, 0, kb + k_of_pair(p))))
        in_specs.append(pl.BlockSpec((1, value_rows + SUM_ROWS, tk), lambda b, p, i: (b, k_of_pair(p), 0)))
        args += [k_arr, vt_arr]
    return pl.pallas_call(
        functools.partial(_pair_attn_kernel, n_src=len(sources)),
        grid=(bsz, n_pairs, t // tq),
        in_specs=in_specs,
        out_specs=pl.BlockSpec((1, tq, LANES), lambda b, p, i: (b, i, p)),
        out_shape=jax.ShapeDtypeStruct((bsz, t, n_pairs * LANES), jnp.bfloat16),
        compiler_params=_params("parallel", "parallel", "arbitrary"),
        name="pair_attention",
    )(*args)


def _diff_attn_kernel(*refs, n_src, lam_init):
    q1_ref, q2_ref, lam_ref, g_ref = refs[:4]
    kv_refs = refs[4:4 + 4 * n_src]
    o_ref = refs[4 + 4 * n_src]
    tq = q1_ref.shape[1]
    qts = (_split_pair_t(q1_ref[0]), _split_pair_t(q2_ref[0]))
    sources = [[(kv_refs[4 * s].shape[1], (kv_refs[4 * s + i], kv_refs[4 * s + 2], kv_refs[4 * s + 3]))
                for s in range(n_src)] for i in range(2)]

    def pv(src, start, size, p):
        return jnp.concatenate([
            jnp.dot(src[1][0, :, pl.ds(start, size)], p[:, :tq], preferred_element_type=jnp.float32),
            jnp.dot(src[2][0, :, pl.ds(start, size)], p[:, tq:], preferred_element_type=jnp.float32)], axis=1)

    def score_of(i):
        return lambda src, start, size: jnp.dot(src[0][0, pl.ds(start, size), :], qts[i],
                                                preferred_element_type=jnp.float32)

    lam = lam_ref[...]
    lam_full = (jnp.exp(jnp.sum(lam[0:1] * lam[1:2], axis=-1, keepdims=True))
                - jnp.exp(jnp.sum(lam[2:3] * lam[3:4], axis=-1, keepdims=True)) + lam_init)

    def finish(o1, o2):
        o = o1 - lam_full * o2
        o = o * lax.rsqrt(jnp.mean(o * o, axis=0, keepdims=True) + EPS) * g_ref[...] * (1.0 - lam_init)
        o_ref[0, :, :LANES] = o[:, :tq].T.astype(o_ref.dtype)
        o_ref[0, :, LANES:] = o[:, tq:].T.astype(o_ref.dtype)

    sms = (_LaggedSoftmax(2 * tq, VT_ROWS), _LaggedSoftmax(2 * tq, VT_ROWS))
    work = [(i, sources[i][s][1], start, size)
            for s in range(n_src) for start, size in _chunks(sources[0][s][0]) for i in range(2)]
    _software_pipeline(
        len(work),
        lambda c: score_of(work[c][0])(*work[c][1:]),
        lambda c, s: sms[work[c][0]].add(pv(*work[c][1:], sms[work[c][0]].probs(s))),
        SCORE_LOOKAHEAD)
    finish(_normalized(sms[0].acc), _normalized(sms[1].acc))

    @pl.when(sms[0].may_overflow() | sms[1].may_overflow())
    def _():
        finish(*[_exact_attention(2 * tq, VT_ROWS, sources[i], score_of(i), pv) for i in range(2)])


def _diff_attention(q_arr, sources, lam, subln_g, lam_init, *, tq):
    bsz, t, _ = q_arr.shape
    n_pairs = 4
    in_specs = [
        pl.BlockSpec((1, tq, LANES), lambda b, p, i: (b, i, p)),
        pl.BlockSpec((1, tq, LANES), lambda b, p, i: (b, i, n_pairs + p)),
        pl.BlockSpec(lam.shape, lambda b, p, i: (0, 0)),
        pl.BlockSpec(subln_g.shape, lambda b, p, i: (0, 0)),
    ]
    args = [q_arr, q_arr, lam, subln_g]
    for k_arr, vt_arr in sources:
        tk = k_arr.shape[1]
        in_specs += [
            pl.BlockSpec((1, tk, LANES), lambda b, p, i: (b, 0, 2 * n_pairs + p)),
            pl.BlockSpec((1, tk, LANES), lambda b, p, i: (b, 0, 3 * n_pairs + p)),
            pl.BlockSpec((1, VT_ROWS, tk), lambda b, p, i: (b, 2 * p, 0)),
            pl.BlockSpec((1, VT_ROWS, tk), lambda b, p, i: (b, 2 * p + 1, 0)),
        ]
        args += [k_arr, k_arr, vt_arr, vt_arr]
    return pl.pallas_call(
        functools.partial(_diff_attn_kernel, n_src=len(sources), lam_init=lam_init),
        grid=(bsz, n_pairs, t // tq),
        in_specs=in_specs,
        out_specs=pl.BlockSpec((1, tq, 2 * LANES), lambda b, p, i: (b, i, p)),
        out_shape=jax.ShapeDtypeStruct((bsz, t, 2 * LANES * n_pairs), jnp.bfloat16),
        compiler_params=_params("parallel", "parallel", "arbitrary"),
        name="diff_attention",
    )(*args)


def _na_attn_kernel(q_ref, k_ref, v_ref, kc_ref, vct_ref, bias_ref, o_ref, *, n_rows):
    step = pl.program_id(2)
    tg = NA_GROUP_ROWS * GRID_W
    n_groups = NA_ROWS_PER_STEP // NA_GROUP_ROWS
    win_rows = NA_WIN_H + NA_GROUP_ROWS - 1
    half = NA_WIN_H // 2
    lane = lax.broadcasted_iota(jnp.int32, (1, 2 * tg), 1)
    first_row = (lane & GRID_W) == 0

    def window(g):
        r0 = step * NA_ROWS_PER_STEP + NA_GROUP_ROWS * g
        rs0 = jnp.clip(r0 - half, 0, n_rows - NA_WIN_H)
        rs1 = jnp.clip(r0 + 1 - half, 0, n_rows - NA_WIN_H)
        ws = jnp.minimum(rs0, n_rows - win_rows)
        return r0, rs0, rs1, ws, pl.multiple_of(ws * GRID_W, GRID_W)

    def queries_t(g):
        return _split_pair_t(q_ref[0, pl.ds(pl.multiple_of(g * tg, tg), tg), :])

    def window_scores(g, qt):
        r0, rs0, rs1, ws, tok0 = window(g)
        pieces = []
        for kr in range(win_rows):
            key_row = ws + kr
            e = jnp.clip(key_row - r0 + NA_WIN_H - 1, 0, 2 * NA_WIN_H - 1)
            out0 = jnp.where((key_row >= rs0) & (key_row < rs0 + NA_WIN_H), 0.0, MASK_VALUE)
            out1 = jnp.where((key_row >= rs1) & (key_row < rs1 + NA_WIN_H), 0.0, MASK_VALUE)
            pieces.append(bias_ref[0, e] + jnp.where(first_row, out0, out1))
        k_nb = k_ref[0, pl.ds(tok0, win_rows * GRID_W), :]
        return jnp.dot(k_nb, qt, preferred_element_type=jnp.float32) + jnp.concatenate(pieces, axis=0)

    def attend(g, s_nb, s_cx, ref):
        tok0 = window(g)[-1]
        vt_nb = jnp.concatenate([v_ref[0, pl.ds(tok0, win_rows * GRID_W), :].T,
                                 jnp.ones((SUM_ROWS, win_rows * GRID_W), jnp.bfloat16)], axis=0)
        p_nb = jnp.exp2(s_nb - ref).astype(jnp.bfloat16)
        p_cx = jnp.exp2(s_cx - ref).astype(jnp.bfloat16)
        acc = (jnp.dot(vt_nb, p_nb, preferred_element_type=jnp.float32)
               + jnp.dot(vct_ref[0], p_cx, preferred_element_type=jnp.float32))
        o_ref[0, pl.ds(pl.multiple_of(g * tg, tg), tg), :] = _merge_pair_t(_normalized(acc)).astype(o_ref.dtype)

    qts = [queries_t(g) for g in range(n_groups)]
    s_cx_all = jnp.dot(kc_ref[0], jnp.concatenate(qts, axis=1), preferred_element_type=jnp.float32)
    jump = [jnp.zeros((1, 2 * tg), jnp.float32)]

    def consume(g, s_nb):
        s_cx = s_cx_all[:, g * 2 * tg:(g + 1) * 2 * tg]
        ref = jnp.max(s_cx, axis=0, keepdims=True)
        jump[0] = jnp.maximum(jump[0], jnp.max(s_nb, axis=0, keepdims=True) - ref)
        attend(g, s_nb, s_cx, ref)

    _software_pipeline(n_groups, lambda g: window_scores(g, qts[g]), consume, NA_LOOKAHEAD)

    @pl.when(jnp.max(jump[0]) > MAX_LAGGED_EXPONENT)
    def _():
        def exact_group(g, carry):
            qt = queries_t(g)
            s_nb = window_scores(g, qt)
            s_cx = jnp.dot(kc_ref[0], qt, preferred_element_type=jnp.float32)
            attend(g, s_nb, s_cx, jnp.maximum(jnp.max(s_nb, axis=0, keepdims=True),
                                              jnp.max(s_cx, axis=0, keepdims=True)))
            return carry

        lax.fori_loop(0, n_groups, exact_group, 0)


def _na_attention(qkv, qk_ctx, vt_ctx, bias):
    bsz, s, _ = qkv.shape
    l_ctx = qk_ctx.shape[1]
    n_pairs = bias.shape[0]
    tq = NA_ROWS_PER_STEP * GRID_W
    return pl.pallas_call(
        functools.partial(_na_attn_kernel, n_rows=s // GRID_W),
        grid=(bsz, n_pairs, s // tq),
        in_specs=[
            pl.BlockSpec((1, tq, LANES), lambda b, p, i: (b, i, p)),
            pl.BlockSpec((1, s, LANES), lambda b, p, i: (b, 0, n_pairs + p)),
            pl.BlockSpec((1, s, LANES), lambda b, p, i: (b, 0, 2 * n_pairs + p)),
            pl.BlockSpec((1, l_ctx, LANES), lambda b, p, i: (b, 0, n_pairs + p)),
            pl.BlockSpec((1, VT_ROWS, l_ctx), lambda b, p, i: (b, p, 0)),
            pl.BlockSpec((1,) + bias.shape[1:], lambda b, p, i: (p, 0, 0, 0)),
        ],
        out_specs=pl.BlockSpec((1, tq, LANES), lambda b, p, i: (b, i, p)),
        out_shape=jax.ShapeDtypeStruct((bsz, s, n_pairs * LANES), jnp.bfloat16),
        compiler_params=_params("parallel", "parallel", "arbitrary"),
        name="na_attention",
    )(qkv, qkv, qkv, qk_ctx, vt_ctx, bias)


def _rope_tables(n):
    t = jnp.arange(n, dtype=jnp.int32)
    row = (t // GRID_W).astype(jnp.float32)
    col = (t % GRID_W).astype(jnp.float32)
    nf = HEAD_DIM // 4
    inv = 1.0 / (ROPE_THETA ** (jnp.arange(nf, dtype=jnp.float32) / nf))
    ar = row[:, None] * inv
    ac = col[:, None] * inv
    zero = jnp.zeros_like(ar)
    cos = jnp.concatenate([jnp.cos(ar), jnp.cos(ar), jnp.cos(ac), jnp.cos(ac)], axis=-1)
    sin_up = jnp.concatenate([-jnp.sin(ar), zero, -jnp.sin(ac), zero], axis=-1)
    sin_down = jnp.concatenate([zero, jnp.sin(ar), zero, jnp.sin(ac)], axis=-1)
    return tuple(jnp.tile(x, (1, LANES // HEAD_DIM)) for x in (cos, sin_up, sin_down))


def _na_bias_table(rpb):
    n_heads, n_rel, _ = rpb.shape
    pad = GRID_W - NA_WIN_W
    w = jnp.pad(rpb, ((0, 0), (0, 0), (pad, pad)))
    t = jnp.stack([w[:, :, GRID_W - 1 - q:2 * GRID_W - 1 - q] for q in range(GRID_W)], axis=2)
    t = jnp.swapaxes(t, 2, 3)
    kcol, qcol = np.arange(GRID_W)[:, None], np.arange(GRID_W)[None, :]
    cstart = np.clip(qcol - NA_WIN_W // 2, 0, GRID_W - NA_WIN_W)
    col_ok = (kcol >= cstart) & (kcol < cstart + NA_WIN_W)
    t = jnp.where(col_ok, t * LOG2_E, MASK_VALUE)
    first = jnp.concatenate([t, t[:, -1:]], axis=1)
    second = jnp.concatenate([t[:, :1], t], axis=1)
    both = jnp.stack([first, second], axis=3)
    both = both.reshape(n_heads // 2, 2, n_rel + 1, GRID_W, 2 * GRID_W).transpose(0, 2, 3, 1, 4)
    return both.reshape(n_heads // 2, n_rel + 1, GRID_W, 4 * GRID_W)


def _dup_heads(w):
    d, n = w.shape
    return jnp.tile(w.reshape(d, n // HEAD_DIM, 1, HEAD_DIM), (1, 1, 2, 1)).reshape(d, 2 * n)


def _tile_gain(g):
    return jnp.tile(g, LANES // HEAD_DIM)


def kernel(x, c, ctx, c_ctx, w_mod, b_mod, norm_g, w_ffn_in, w_ffn_out, na_w_qkv, na_rpb, na_w_o, gqa_w_qkv, gqa_q_gain, gqa_k_gain, gqa_w_o, diff_w_qkv, diff_lam, diff_subln_g, diff_w_o):
    bsz, seq, d = x.shape
    l_ctx = ctx.shape[1]
    depth = w_mod.shape[0]
    bf16 = jnp.bfloat16
    tm = 1024
    tm_ctx = l_ctx
    tq = 1024
    n_pairs = d // LANES

    n_cond = 8 * ((bsz + 1 + 7) // 8)
    cond = jnp.concatenate([c, c_ctx[None], jnp.zeros((n_cond - bsz - 1, d), c.dtype)], axis=0)
    mod_all = _modulation(cond, w_mod, b_mod).reshape(depth, n_cond, N_MOD, d)
    ctx_row = bsz

    rope = _rope_tables(seq)
    no_rope = tuple(jnp.zeros((l_ctx, LANES), jnp.float32) for _ in range(3))
    no_gain = jnp.ones((2, LANES), jnp.float32)
    value_t = _Ep(token_major=False, value_rows=LANES)

    w_in_all, w_out_all = w_ffn_in.astype(bf16), w_ffn_out.astype(bf16)
    na_w_all, na_wo_all = na_w_qkv.astype(bf16), na_w_o.astype(bf16)
    gqa_wo_all = gqa_w_o.astype(bf16)
    diff_w_all, diff_wo_all = diff_w_qkv.astype(bf16), diff_w_o.astype(bf16)

    h, hc = x, ctx
    for i in range(depth):
        kind, j, last = i % 3, i // 3, i == depth - 1
        mod, g = (mod_all, (i,)), (norm_g, (i,))
        win, wout = [(w_in_all, (i, k)) for k in range(2)], [(w_out_all, (i, k)) for k in range(2)]

        h = _ffn(h, mod, None, g, win[0], wout[0], which=0, tm=tm)
        hc = _ffn(hc, mod, ctx_row, g, win[0], wout[0], which=0, tm=tm_ctx)

        if kind == 0:
            w = (na_w_all, (j,))
            ep = [_Ep(qscale=True)] * n_pairs + [_Ep()] * (2 * n_pairs)
            ep_c = [_Ep(qscale=True)] * n_pairs + [_Ep()] * n_pairs + [value_t] * n_pairs
            qkv, _ = _project(h, mod, None, g, w, no_gain, rope, ep, tm=tm)
            qkv_c, vt_c = _project(hc, mod, ctx_row, g, w, no_gain, no_rope, ep_c, tm=tm_ctx)
            o = _na_attention(qkv, qkv_c, vt_c, _na_bias_table(na_rpb[j]))
            if not last:
                oc = _pair_attention(qkv_c, 0, [(qkv_c, n_pairs, vt_c)], n_pairs, lambda p: p, tq=l_ctx)
            w_o = (na_wo_all, (j,))
        elif kind == 1:
            wf = gqa_w_qkv[j]
            dkv = (wf.shape[1] - d) // 2
            w = (jnp.concatenate([wf[:, :d], _dup_heads(wf[:, d:d + dkv]), wf[:, d + dkv:]], axis=1).astype(bf16), ())
            gains = jnp.stack([_tile_gain(gqa_q_gain[j]), _tile_gain(gqa_k_gain[j])])
            n_kv = dkv // HEAD_DIM
            rep_pairs = n_pairs // n_kv
            shared_v = [_Ep(token_major=False, value_rows=HEAD_DIM)] * (dkv // LANES)
            ep = [_Ep(norm=0, rope=True, qscale=True)] * n_pairs + [_Ep(norm=1, rope=True)] * n_kv + shared_v
            ep_c = [_Ep(norm=0, qscale=True)] * n_pairs + [_Ep(norm=1)] * n_kv + shared_v
            qk, vt = _project(h, mod, None, g, w, gains, rope, ep, tm=tm)
            qk_c, vt_c = _project(hc, mod, ctx_row, g, w, gains, no_rope, ep_c, tm=tm_ctx)
            o = _pair_attention(qk, 0, [(qk, n_pairs, vt), (qk_c, n_pairs, vt_c)], n_pairs,
                                lambda p: p // rep_pairs, tq=tq, value_rows=HEAD_DIM)
            if not last:
                oc = _pair_attention(qk_c, 0, [(qk_c, n_pairs, vt_c)], n_pairs, lambda p: p // rep_pairs,
                                     tq=l_ctx, value_rows=HEAD_DIM)
            w_o = (gqa_wo_all, (j,))
        else:
            w = (diff_w_all, (j,))
            lam_init = 0.8 - 0.6 * math.exp(-0.3 * i)
            ep = [_Ep(rope=True, qscale=True)] * n_pairs + [_Ep(rope=True)] * n_pairs + [value_t] * n_pairs
            ep_c = [_Ep(qscale=True)] * n_pairs + [_Ep()] * n_pairs + [value_t] * n_pairs
            qk, vt = _project(h, mod, None, g, w, no_gain, rope, ep, tm=tm)
            qk_c, vt_c = _project(hc, mod, ctx_row, g, w, no_gain, no_rope, ep_c, tm=tm_ctx)
            sub_g = diff_subln_g[j][:, None]
            o = _diff_attention(qk, [(qk, vt), (qk_c, vt_c)], diff_lam[j], sub_g, lam_init, tq=tq)
            if not last:
                oc = _diff_attention(qk_c, [(qk_c, vt_c)], diff_lam[j], sub_g, lam_init, tq=l_ctx)
            w_o = (diff_wo_all, (j,))

        h = _ffn(h, mod, None, g, win[1], wout[1], which=1, tm=tm, attn=o, w_o=w_o)
        if not last:
            hc = _ffn(hc, mod, ctx_row, g, win[1], wout[1], which=1, tm=tm_ctx, attn=oc, w_o=w_o)
    return h
```
